```python
import math
import jax, jax.numpy as jnp
from jax import lax
import numpy as np

D_MODEL = 1024
BATCH = 8
SEQ = 4096
DEPTH = 2

CHUNK = 64
N_MIXERS = 2
EPS = 1e-6

A_HEADS = 16
A_HEAD_DIM = 64
A_Q_RANK = 256
A_KV_RANK = 128
IDX_HEADS = 8
IDX_DIM = 64
TOPK_MAX = 256
Q_BLOCK = 128
A_PROJ = A_Q_RANK + A_KV_RANK + IDX_DIM + IDX_HEADS

B_HEAD = 64
B_HEADS = D_MODEL // B_HEAD
DECAY_RANK = 64
AAA_RANK = 64
GATE_RANK = 128
GN_EPS = 64e-5

D_FF = 2816
CONV_W = 3

N_A = (DEPTH + 1) // 2
N_B = DEPTH // 2

kernel_name = 'hybrid_dsa_rwkv7_convffn_adaln'


def rms_norm(x, g, eps=EPS):
    xf = x.astype(jnp.float32)
    y = xf * lax.rsqrt(jnp.mean(xf * xf, axis=-1, keepdims=True) + eps)
    return (y * g.astype(jnp.float32)).astype(x.dtype)


def layer_norm(x, g, b, eps=EPS):
    xf = x.astype(jnp.float32)
    mu = jnp.mean(xf, axis=-1, keepdims=True)
    var = jnp.mean(jnp.square(xf - mu), axis=-1, keepdims=True)
    y = (xf - mu) * lax.rsqrt(var + eps)
    return (y * g.astype(jnp.float32) + b.astype(jnp.float32)).astype(x.dtype)


def alibi_slopes(n):
    return jnp.power(2.0, -8.0 * jnp.arange(1, n + 1, dtype=jnp.float32) / n)


def dsa_mixer(h, w_in, q_norm, kv_norm, kidx_g, kidx_b, w_uq, w_qi, w_uk, w_uv, w_o):
    B, S, _ = h.shape
    proj = h @ w_in
    s1, s2, s3 = A_Q_RANK, A_Q_RANK + A_KV_RANK, A_Q_RANK + A_KV_RANK + IDX_DIM
    q_lat, c_kv, k_idx, w_idx = proj[..., :s1], proj[..., s1:s2], proj[..., s2:s3], proj[..., s3:]
    q_lat = rms_norm(q_lat, q_norm)
    c_kv = rms_norm(c_kv, kv_norm)
    k_idx = layer_norm(k_idx, kidx_g, kidx_b)
    q = (q_lat @ w_uq).reshape(B, S, A_HEADS, A_HEAD_DIM)
    q_abs = jnp.einsum('bshd,hcd->bshc', q, w_uk) * (A_HEAD_DIM ** -0.5)
    q_idx = (q_lat @ w_qi).reshape(B, S, IDX_HEADS, IDX_DIM)
    w_idx = w_idx * (IDX_HEADS ** -0.5 * IDX_DIM ** -0.5)

    k_top = min(TOPK_MAX, S // 4)
    key_chunk = jnp.arange(S) // CHUNK
    slopes = alibi_slopes(A_HEADS)
    nb = S // Q_BLOCK

    def to_blocks(t):
        return t.reshape(B, nb, Q_BLOCK, *t.shape[2:]).swapaxes(0, 1)

    def block(args):
        qa, qi, wi, start = args
        qpos = start + jnp.arange(Q_BLOCK)
        qchunk = qpos // CHUNK
        rel = jax.nn.relu(jnp.einsum('bqhd,bsd->bqhs', qi, k_idx))
        score = jnp.einsum('bqh,bqhs->bqs', wi, rel).astype(jnp.float32)
        admissible = key_chunk[None, :] <= qchunk[:, None]
        score = jnp.where(admissible[None], score, -jnp.inf)
        _, idx = lax.top_k(score, k_top)
        ckv_sel = jax.vmap(lambda cb, ib: cb[ib])(c_kv, idx)
        valid = (idx // CHUNK) <= qchunk[None, :, None]
        dist = jnp.abs(qpos[None, :, None] - idx).astype(jnp.float32)
        logits = jnp.einsum('bqhc,bqkc->bqhk', qa, ckv_sel).astype(jnp.float32)
        logits = logits - slopes[None, None, :, None] * dist[:, :, None, :]
        logits = jnp.where(valid[:, :, None, :], logits, -jnp.inf)
        p = jax.nn.softmax(logits, axis=-1).astype(ckv_sel.dtype)
        return jnp.einsum('bqhk,bqkc->bqhc', p, ckv_sel)

    starts = jnp.arange(nb) * Q_BLOCK
    o_lat = lax.map(block, (to_blocks(q_abs), to_blocks(q_idx), to_blocks(w_idx), starts))
    o_lat = o_lat.swapaxes(0, 1).reshape(B, S, A_HEADS, A_KV_RANK)
    o = jnp.einsum('bshc,hcd->bshd', o_lat, w_uv).reshape(B, S, A_HEADS * A_HEAD_DIM)
    return o @ w_o


def rwkv7_mixer(h, mu, w_r, w_k, w_v, w_o, w0, w1, w2, a0, a1, a2, g1, g2,
                k_k, k_a, r_k, gn_g, gn_b):
    B, S, D = h.shape
    H, N = B_HEADS, B_HEAD
    f32 = jnp.float32
    h_prev = jnp.pad(h, ((0, 0), (1, 0), (0, 0)))[:, :-1]
    dx = h_prev - h
    xr, xw, xk, xv, xa, xg = h + dx * mu[:, None, None, :]
    r = xr @ w_r
    k = xk @ w_k
    v = xv @ w_v
    w_log = -jax.nn.softplus(-(w0 + jnp.tanh(xw @ w1) @ w2)) - 0.5
    decay = jnp.exp(-jnp.exp(w_log.astype(f32)))
    a = jax.nn.sigmoid(a0 + (xa @ a1) @ a2)
    g = jax.nn.sigmoid(xg @ g1) @ g2
    kk = (k * k_k).reshape(B, S, H, N).astype(f32)
    kk = kk / jnp.maximum(jnp.sqrt(jnp.sum(kk * kk, axis=-1, keepdims=True)), 1e-12)
    k = k * (1 + (a - 1) * k_a)

    def seq_major(t):
        return t.reshape(B, S, H, N).transpose(1, 0, 2, 3).astype(f32)

    xs = (seq_major(r), seq_major(decay), seq_major(k), seq_major(v),
          kk.transpose(1, 0, 2, 3), seq_major(a))

    def step(state, inp):
        r_t, w_t, k_t, v_t, kk_t, a_t = inp
        sa = jnp.einsum('bhvk,bhk->bhv', state, -kk_t)
        state = (state * w_t[:, :, None, :]
                 + sa[..., None] * (kk_t * a_t)[:, :, None, :]
                 + v_t[..., None] * k_t[:, :, None, :])
        y = jnp.einsum('bhvk,bhk->bhv', state, r_t)
        return state, y

    state0 = jnp.zeros((B, H, N, N), f32)
    _, y = lax.scan(step, state0, xs)
    y = y.transpose(1, 0, 2, 3)
    mu_y = jnp.mean(y, axis=-1, keepdims=True)
    var_y = jnp.mean(jnp.square(y - mu_y), axis=-1, keepdims=True)
    y = (y - mu_y) * lax.rsqrt(var_y + GN_EPS)
    y = y * gn_g.reshape(H, N).astype(f32) + gn_b.reshape(H, N).astype(f32)
    rh, kh, vh = (t.reshape(B, S, H, N).astype(f32) for t in (r, k, v))
    y = y + jnp.sum(rh * kh * r_k.astype(f32), axis=-1, keepdims=True) * vh
    y = y.reshape(B, S, D).astype(h.dtype) * g
    return y @ w_o


def conv_ffn(h, w_up, conv_w, conv_b, w_down):
    u = h @ w_up
    c2 = u.shape[-1]
    u = lax.conv_general_dilated(u, conv_w[:, None, :].astype(u.dtype), window_strides=(1,),
                                 padding=[(CONV_W - 1, 0)],
                                 dimension_numbers=('NWC', 'WIO', 'NWC'),
                                 feature_group_count=c2) + conv_b
    ug, uv = u[..., :D_FF], u[..., D_FF:]
    return (jax.nn.silu(ug) * uv) @ w_down


def setup_inputs(seed: int = 0) -> dict:
    key = jax.random.key(seed)
    ks = iter(jax.random.split(key, 48))
    f32 = jnp.float32
    D = D_MODEL
    H, dh, C = A_HEADS, A_HEAD_DIM, A_KV_RANK

    def nrm(shape, scale):
        return jax.random.normal(next(ks), shape, f32) * scale

    def gain(shape):
        return 1.0 + nrm(shape, 0.05)

    return {
        'x': nrm((BATCH, SEQ, D), 1.0),
        'c': nrm((BATCH, D), 1.0),
        'ada_w': nrm((DEPTH, D, 6 * D), 0.5 * D ** -0.5),
        'ada_b': nrm((DEPTH, 6 * D), 0.02),
        'norm_mix': gain((DEPTH, D)),
        'norm_ffn': gain((DEPTH, D)),
        'norm_final': gain((D,)),
        'a_w_in': nrm((N_A, D, A_PROJ), D ** -0.5),
        'a_q_norm': gain((N_A, A_Q_RANK)),
        'a_kv_norm': gain((N_A, C)),
        'a_kidx_g': gain((N_A, IDX_DIM)),
        'a_kidx_b': nrm((N_A, IDX_DIM), 0.02),
        'a_w_uq': nrm((N_A, A_Q_RANK, H * dh), A_Q_RANK ** -0.5),
        'a_w_qi': nrm((N_A, A_Q_RANK, IDX_HEADS * IDX_DIM), A_Q_RANK ** -0.5),
        'a_w_uk': nrm((N_A, H, C, dh), C ** -0.5),
        'a_w_uv': nrm((N_A, H, C, dh), C ** -0.5),
        'a_w_o': nrm((N_A, H * dh, D), (H * dh) ** -0.5),
        'b_mu': jax.random.uniform(next(ks), (N_B, 6, D), f32),
        'b_w_r': nrm((N_B, D, D), D ** -0.5),
        'b_w_k': nrm((N_B, D, D), D ** -0.5),
        'b_w_v': nrm((N_B, D, D), D ** -0.5),
        'b_w_o': nrm((N_B, D, D), D ** -0.5),
        'b_w0': nrm((N_B, D), 0.5),
        'b_w1': nrm((N_B, D, DECAY_RANK), D ** -0.5),
        'b_w2': nrm((N_B, DECAY_RANK, D), 0.5 * DECAY_RANK ** -0.5),
        'b_a0': nrm((N_B, D), 0.1),
        'b_a1': nrm((N_B, D, AAA_RANK), D ** -0.5),
        'b_a2': nrm((N_B, AAA_RANK, D), 0.5 * AAA_RANK ** -0.5),
        'b_g1': nrm((N_B, D, GATE_RANK), D ** -0.5),
        'b_g2': nrm((N_B, GATE_RANK, D), GATE_RANK ** -0.5),
        'b_k_k': 0.85 + nrm((N_B, D), 0.05),
        'b_k_a': gain((N_B, D)),
        'b_r_k': nrm((N_B, B_HEADS, B_HEAD), 0.1),
        'b_gn_g': gain((N_B, D)),
        'b_gn_b': nrm((N_B, D), 0.02),
        'f_w_up': nrm((DEPTH, D, 2 * D_FF), D ** -0.5),
        'f_conv_w': nrm((DEPTH, CONV_W, 2 * D_FF), 0.6),
        'f_conv_b': nrm((DEPTH, 2 * D_FF), 0.02),
        'f_w_down': nrm((DEPTH, D_FF, D), D_FF ** -0.5),
    }


def reference(x, c, ada_w, ada_b, norm_mix, norm_ffn, norm_final,
              a_w_in, a_q_norm, a_kv_norm, a_kidx_g, a_kidx_b, a_w_uq, a_w_qi, a_w_uk, a_w_uv, a_w_o,
              b_mu, b_w_r, b_w_k, b_w_v, b_w_o, b_w0, b_w1, b_w2, b_a0, b_a1, b_a2, b_g1, b_g2,
              b_k_k, b_k_a, b_r_k, b_gn_g, b_gn_b,
              f_w_up, f_conv_w, f_conv_b, f_w_down):
    h = x
    c_act = jax.nn.silu(c)
    for i in range(DEPTH):
        mod = (c_act @ ada_w[i] + ada_b[i])[:, None, :]
        sh_m, sc_m, g_m, sh_f, sc_f, g_f = jnp.split(mod, 6, axis=-1)
        hn = rms_norm(h, norm_mix[i]) * (1 + sc_m) + sh_m
        j = i // N_MIXERS
        if i % N_MIXERS == 0:
            y = dsa_mixer(hn, a_w_in[j], a_q_norm[j], a_kv_norm[j], a_kidx_g[j], a_kidx_b[j],
                          a_w_uq[j], a_w_qi[j], a_w_uk[j], a_w_uv[j], a_w_o[j])
        else:
            y = rwkv7_mixer(hn, b_mu[j], b_w_r[j], b_w_k[j], b_w_v[j], b_w_o[j],
                            b_w0[j], b_w1[j], b_w2[j], b_a0[j], b_a1[j], b_a2[j],
                            b_g1[j], b_g2[j], b_k_k[j], b_k_a[j], b_r_k[j], b_gn_g[j], b_gn_b[j])
        h = h + g_m * y
        hn = rms_norm(h, norm_ffn[i]) * (1 + sc_f) + sh_f
        h = h + g_f * conv_ffn(hn, f_w_up[i], f_conv_w[i], f_conv_b[i], f_w_down[i])
    return rms_norm(h, norm_final)
```

```python
import functools

import numpy as np
import jax
import jax.numpy as jnp
from jax import lax
from jax.experimental import pallas as pl
from jax.experimental.pallas import tpu as pltpu

F32 = jnp.float32
BF16 = jnp.bfloat16
I32 = jnp.int32

EPS = 1e-6
CHUNK = 64
A_HEADS = 16
A_HEAD_DIM = 64
A_Q_RANK = 256
A_KV_RANK = 128
IDX_HEADS = 8
IDX_DIM = 64
TOPK_MAX = 256
B_HEAD = 64
GN_EPS = 64e-5
CONV_W = 3

LANES = 128
SUBLANES = 8
VMEM_LIMIT = 56 * 1024 * 1024
NEG = -1e30
INT_MIN = -(2 ** 31)

TQ = 128
KB = 512
SCAN_L = 64
NT = (((1,), (1,)), ((), ()))


def _dot(a, b):
    return jnp.dot(a, b, preferred_element_type=F32)


def _dot_nt(a, b):
    return lax.dot_general(a, b, NT, preferred_element_type=F32)


def _split(x):
    hi = x.astype(BF16)
    lo = (x - hi.astype(F32)).astype(BF16)
    return hi, lo


def _dot3(a, b):
    ah, al = _split(a)
    bh, bl = _split(b)
    return _dot(ah, bh) + (_dot(ah, bl) + _dot(al, bh))


def _sigmoid(x):
    return 1.0 / (1.0 + jnp.exp(-x))


def _rms(x, gain):
    return x * lax.rsqrt(jnp.mean(x * x, axis=-1, keepdims=True) + EPS) * gain


def _rms_mod(x, gain, sc, sh):
    return _rms(x, gain) * (1.0 + sc) + sh


def _params(*sem):
    return pltpu.CompilerParams(dimension_semantics=sem, vmem_limit_bytes=VMEM_LIMIT)


def _full(shape):
    n = len(shape)
    return pl.BlockSpec(shape, lambda *_: (0,) * n)


def _ada_body(c_ref, w_ref, b_ref, o_ref):
    c = c_ref[...]
    o_ref[0] = _dot(c * _sigmoid(c), w_ref[0]) + b_ref[0]


def _ada(c, ada_w, ada_b):
    depth, d, n = ada_w.shape
    b = c.shape[0]
    tn = n // 4
    return pl.pallas_call(
        _ada_body,
        grid=(depth, n // tn),
        in_specs=[pl.BlockSpec((b, d), lambda i, j: (0, 0)),
                  pl.BlockSpec((1, d, tn), lambda i, j: (i, 0, j)),
                  pl.BlockSpec((1, 1, tn), lambda i, j: (i, 0, j))],
        out_specs=pl.BlockSpec((1, b, tn), lambda i, j: (i, 0, j)),
        out_shape=jax.ShapeDtypeStruct((depth, b, n), F32),
        name="ada_mod",
        compiler_params=_params("arbitrary", "arbitrary"),
    )(c, ada_w, ada_b.reshape(depth, 1, n))


def _ffn_body(final_norm, nf, h_ref, sh_ref, sc_ref, g_ref, gain_ref, wup_ref, cw_ref, cb_ref,
              wdn_ref, gfin_ref, o_ref, carry_ref, hn_ref, acc_ref):
    tm = h_ref.shape[1]

    @pl.when(pl.program_id(1) == 0)
    def _():
        carry_ref[...] = jnp.zeros_like(carry_ref)

    hn_ref[...] = _rms_mod(h_ref[0], gain_ref[...], sc_ref[0], sh_ref[0]).astype(BF16)
    acc_ref[...] = jnp.zeros_like(acc_ref)
    rid = lax.broadcasted_iota(I32, (tm, 1), 0)

    def conv(part, j):
        u = _dot(hn_ref[...], wup_ref[part, j])
        prev = carry_ref[part, j]
        carry_ref[part, j] = u[tm - SUBLANES:, :]
        p1 = prev[SUBLANES - 1:SUBLANES, :]
        p2 = prev[SUBLANES - 2:SUBLANES - 1, :]
        s1 = jnp.where(rid == 0, p1, pltpu.roll(u, 1, 0))
        s2 = jnp.where(rid == 0, p2, jnp.where(rid == 1, p1, pltpu.roll(u, 2, 0)))
        cw = cw_ref[part, j]
        return cw[2:3] * u + cw[1:2] * s1 + cw[0:1] * s2 + cb_ref[part, j]

    def chunk(j, carry):
        ug = conv(0, j)
        uv = conv(1, j)
        act = (ug * _sigmoid(ug) * uv).astype(BF16)
        acc_ref[...] += _dot(act, wdn_ref[j])
        return carry

    lax.fori_loop(0, nf, chunk, 0)
    out = h_ref[0] + g_ref[0] * acc_ref[...]
    if final_norm:
        out = _rms(out, gfin_ref[...])
    o_ref[0] = out


def _ffn(h, sh, sc, g, gain, w_up, conv_w, conv_b, w_down, gfin, final_norm, tm=512, tf=256):
    b, s, d = h.shape
    f = w_down.shape[0]
    nf = f // tf
    tm = min(tm, s)
    wup = w_up.astype(BF16).reshape(d, 2, nf, tf).transpose(1, 2, 0, 3)
    cw = conv_w.reshape(CONV_W, 2, nf, tf).transpose(1, 2, 0, 3)
    cb = conv_b.reshape(2, nf, 1, tf)
    wdn = w_down.astype(BF16).reshape(nf, tf, d)
    row = pl.BlockSpec((1, 1, d), lambda bi, i: (bi, 0, 0))
    tile = pl.BlockSpec((1, tm, d), lambda bi, i: (bi, i, 0))
    return pl.pallas_call(
        functools.partial(_ffn_body, final_norm, nf),
        grid=(b, s // tm),
        in_specs=[tile, row, row, row, _full((1, d)), _full(wup.shape), _full(cw.shape),
                  _full(cb.shape), _full(wdn.shape), _full((1, d))],
        out_specs=tile,
        out_shape=jax.ShapeDtypeStruct((b, s, d), F32),
        scratch_shapes=[pltpu.VMEM((2, nf, SUBLANES, tf), F32),
                        pltpu.VMEM((tm, d), BF16),
                        pltpu.VMEM((tm, d), F32)],
        name="conv_ffn",
        compiler_params=_params("arbitrary", "arbitrary"),
    )(h, sh, sc, g, gain.reshape(1, d), wup, cw, cb, wdn, gfin.reshape(1, d))


def _fold_body(wuq_ref, wuk_ref, wuv_ref, wo_ref, wq_ref, wvo_ref):
    hi = lax.Precision.HIGHEST
    wq_ref[0] = lax.dot_general(wuq_ref[0], wuk_ref[0], NT, precision=hi,
                                preferred_element_type=F32) * (A_HEAD_DIM ** -0.5)
    wvo_ref[0] = jnp.dot(wuv_ref[0], wo_ref[0], precision=hi, preferred_element_type=F32)


def _fold(w_uq, w_uk, w_uv, w_o):
    rq = w_uq.shape[0]
    h, c, dh = w_uk.shape
    d = w_o.shape[1]
    wuq = w_uq.reshape(rq, h, dh).transpose(1, 0, 2)
    wo = w_o.reshape(h, dh, d)
    per_head = lambda *shape: pl.BlockSpec((1,) + shape, lambda i: (i, 0, 0))
    wq, wvo = pl.pallas_call(
        _fold_body,
        grid=(h,),
        in_specs=[per_head(rq, dh), per_head(c, dh), per_head(c, dh), per_head(dh, d)],
        out_specs=[per_head(rq, c), per_head(c, d)],
        out_shape=[jax.ShapeDtypeStruct((h, rq, c), F32), jax.ShapeDtypeStruct((h, c, d), F32)],
        name="dsa_fold",
        compiler_params=_params("arbitrary"),
    )(wuq, w_uk, w_uv, wo)
    return (wq.transpose(1, 0, 2).reshape(rq, h * c).astype(BF16),
            wvo.reshape(h * c, d).astype(BF16))


def _dsa_proj_body(x_ref, sh_ref, sc_ref, gain_ref, win_ref, qn_ref, kvn_ref, kg_ref, kb_ref,
                   wq_ref, wqi_ref, qabs_ref, qidx_ref, aux_ref, kpad_ref, caug_ref):
    tm = x_ref.shape[1]
    hn = _rms_mod(x_ref[0], gain_ref[...], sc_ref[0], sh_ref[0]).astype(BF16)
    proj = _dot(hn, win_ref[...])
    ql = _rms(proj[:, :A_Q_RANK], qn_ref[...]).astype(BF16)
    ck = _rms(proj[:, A_Q_RANK:A_Q_RANK + A_KV_RANK], kvn_ref[...])
    slab = proj[:, A_Q_RANK + A_KV_RANK:]
    lane = lax.broadcasted_iota(I32, (1, LANES), 1)
    is_k = lane < IDX_DIM
    mu = jnp.sum(jnp.where(is_k, slab, 0.0), axis=-1, keepdims=True) * (1.0 / IDX_DIM)
    dk = jnp.where(is_k, slab - mu, 0.0)
    var = jnp.sum(dk * dk, axis=-1, keepdims=True) * (1.0 / IDX_DIM)
    kpad_ref[0] = (dk * lax.rsqrt(var + EPS) * kg_ref[...] + kb_ref[...]).astype(BF16)
    aux_ref[0] = slab * (IDX_HEADS ** -0.5 * IDX_DIM ** -0.5)
    qabs_ref[0] = _dot(ql, wq_ref[...]).astype(BF16)
    qidx_ref[0] = _dot(ql, wqi_ref[...]).astype(BF16)
    pos = pl.program_id(1) * tm + lax.broadcasted_iota(I32, (tm, 1), 0)
    p_hi = (pos >> 6).astype(F32)
    p_lo = (pos & 63).astype(F32)
    extra = jnp.where(lane < 2, p_hi, jnp.where(lane < 4, p_lo, jnp.where(lane == 4, 1.0, 0.0)))
    caug_ref[0, :, :A_KV_RANK] = ck.astype(BF16)
    caug_ref[0, :, A_KV_RANK:] = extra.astype(BF16)


def _dsa_proj(x, sh, sc, gain, w_in, q_norm, kv_norm, kidx_g, kidx_b, wq, w_qi, tm=512):
    b, s, d = x.shape
    tm = min(tm, s)
    n_in = w_in.shape[1]
    win = jnp.pad(w_in, ((0, 0), (0, 4 * LANES - n_in))).astype(BF16)
    wqi = jnp.pad(w_qi.reshape(A_Q_RANK, IDX_HEADS, IDX_DIM),
                  ((0, 0), (0, 0), (0, LANES - IDX_DIM))).reshape(A_Q_RANK, IDX_HEADS * LANES).astype(BF16)
    kg = jnp.pad(kidx_g, (0, LANES - IDX_DIM)).reshape(1, LANES)
    kb = jnp.pad(kidx_b, (0, LANES - IDX_DIM)).reshape(1, LANES)
    row = pl.BlockSpec((1, 1, d), lambda bi, i: (bi, 0, 0))
    tile = lambda n: pl.BlockSpec((1, tm, n), lambda bi, i: (bi, i, 0))
    nq, ni = A_HEADS * A_KV_RANK, IDX_HEADS * LANES
    return pl.pallas_call(
        _dsa_proj_body,
        grid=(b, s // tm),
        in_specs=[tile(d), row, row, _full((1, d)), _full(win.shape), _full((1, A_Q_RANK)),
                  _full((1, A_KV_RANK)), _full((1, LANES)), _full((1, LANES)),
                  _full(wq.shape), _full(wqi.shape)],
        out_specs=[tile(nq), tile(ni), tile(LANES), tile(LANES), tile(2 * A_KV_RANK)],
        out_shape=[jax.ShapeDtypeStruct((b, s, nq), BF16), jax.ShapeDtypeStruct((b, s, ni), BF16),
                   jax.ShapeDtypeStruct((b, s, LANES), F32), jax.ShapeDtypeStruct((b, s, LANES), BF16),
                   jax.ShapeDtypeStruct((b, s, 2 * A_KV_RANK), BF16)],
        name="dsa_proj",
        compiler_params=_params("arbitrary", "arbitrary"),
    )(x, sh, sc, gain.reshape(1, d), win, q_norm.reshape(1, -1), kv_norm.reshape(1, -1), kg, kb, wq, wqi)


def _bf16_digits(v):
    hi = float(np.asarray(v, np.float32).astype(BF16).astype(np.float32))
    lo = float(np.asarray(v - hi, np.float32).astype(BF16).astype(np.float32))
    return hi, lo


def _dsa_attn_body(x_ref, gm_ref, qabs_ref, qidx_ref, aux_ref, kpad_ref, caug_ref, wvo_ref, o_ref,
                   keys_ref, qaug_ref, qis_ref, tab_ref, t_ref, m_ref, al_ref, acc_ref, s_ref, p_ref):
    tq = x_ref.shape[1]
    seq = kpad_ref.shape[1]
    c = A_KV_RANK
    q0 = pl.program_id(1) * tq
    kend = q0 + tq
    n_kb = (kend + KB - 1) // KB
    n128 = n_kb * (KB // LANES)
    lane = lax.broadcasted_iota(I32, (1, LANES), 1)
    trow = lax.broadcasted_iota(I32, (tq, 1), 0)
    slopes = [2.0 ** (-8.0 * (h + 1) / A_HEADS) for h in range(A_HEADS)]

    rel = jnp.maximum(lax.broadcasted_iota(I32, (tq, tq), 1) - trow, 0).astype(F32)
    adm_d = (lax.broadcasted_iota(I32, (tq, tq), 1) >> 6) <= (trow >> 6)
    for h in range(A_HEADS):
        rows = slice(h * tq, (h + 1) * tq)
        hi, lo = _bf16_digits(slopes[h])
        digits = jnp.where(lane == 0, 64.0 * hi, jnp.where(lane == 1, 64.0 * lo,
                           jnp.where(lane == 2, hi, jnp.where(lane == 3, lo, 0.0))))
        qaug_ref[rows, :c] = qabs_ref[0, :, h * c:(h + 1) * c]
        qaug_ref[rows, c:] = jnp.broadcast_to(digits, (tq, LANES)).astype(BF16)
        tab_ref[rows, :] = jnp.where(adm_d, (-2.0 * slopes[h]) * rel, NEG)
    for h in range(IDX_HEADS):
        qis_ref[h * tq:(h + 1) * tq, :] = qidx_ref[0, :, h * LANES:(h + 1) * LANES]

    qchunk = (q0 + trow) >> 6
    aux = aux_ref[0]

    def score_block(kb, carry):
        off = pl.multiple_of(kb * KB, KB)
        z = _dot_nt(qis_ref[...], kpad_ref[0, pl.ds(off, KB), :])
        sc = jnp.zeros((tq, KB), F32)
        for h in range(IDX_HEADS):
            w = aux[:, IDX_DIM + h:IDX_DIM + h + 1]
            sc = sc + w * jnp.maximum(z[h * tq:(h + 1) * tq, :], 0.0)
        kchunk = (off + lax.broadcasted_iota(I32, (1, KB), 1)) >> 6
        sc = jnp.where(kchunk <= qchunk, sc, -jnp.inf)
        bits = pltpu.bitcast(sc, I32)
        key = bits ^ ((bits >> 31) & 0x7FFFFFFF)
        for j in range(KB // LANES):
            keys_ref[kb * (KB // LANES) + j] = key[:, j * LANES:(j + 1) * LANES]
        return carry

    lax.fori_loop(0, n_kb, score_block, 0)

    def count(pred):
        def body(j, acc):
            return acc + jnp.where(pred(keys_ref[j], j), 1.0, 0.0)
        part = lax.fori_loop(0, n128, body, jnp.zeros((tq, LANES), F32))
        return jnp.sum(part, axis=-1, keepdims=True)

    t_ref[...] = jnp.full(t_ref.shape, INT_MIN, I32)

    @pl.when(kend > TOPK_MAX)
    def _():
        kf = float(TOPK_MAX)
        n_pos = count(lambda k, j: k >= 0)
        t0 = jnp.where(n_pos >= kf, 0, INT_MIN).astype(I32)

        def bit_step(i, t):
            cand = t | jnp.left_shift(jnp.int32(1), 30 - i)
            return jnp.where(count(lambda k, j: k >= cand) >= kf, cand, t)

        t = lax.fori_loop(0, 31, bit_step, t0)
        t_ref[...] = jnp.broadcast_to(t, t_ref.shape)
        n_ge = count(lambda k, j: k >= t)

        @pl.when(jnp.max(n_ge) > kf)
        def _():
            need = kf - count(lambda k, j: k > t)
            col = lambda j: j * LANES + lane
            n_bits = max(1, int(seq - 1).bit_length())

            def idx_step(i, x):
                cand = x | jnp.left_shift(jnp.int32(1), n_bits - 1 - i)
                below = count(lambda k, j: (k == t) & (col(j) < cand))
                return jnp.where(below < need, cand, x)

            last = lax.fori_loop(0, n_bits, idx_step, jnp.zeros((tq, 1), I32))
            demoted = jnp.where(t == INT_MIN, t, t - 1)

            def demote(j, carry):
                k = keys_ref[j]
                keys_ref[j] = jnp.where((k == t) & (col(j) > last), demoted, k)
                return carry

            lax.fori_loop(0, n128, demote, 0)

    m_ref[...] = jnp.full(m_ref.shape, NEG, F32)
    acc_ref[...] = jnp.zeros_like(acc_ref)
    thr = t_ref[:, 0:1]

    def attend(off, ts, diag):
        cb = caug_ref[0, pl.ds(off, ts), :]
        s_ref[:, :ts] = _dot_nt(qaug_ref[...], cb)
        j0 = off // LANES
        key = jnp.concatenate([keys_ref[j0 + j] for j in range(ts // LANES)], axis=1)
        bias = jnp.where(key >= thr, 0.0, NEG)
        for h in range(A_HEADS):
            rows = slice(h * tq, (h + 1) * tq)
            logit = s_ref[rows, :ts] + bias
            if diag:
                logit = logit + tab_ref[rows, :]
            m_old = m_ref[rows, :]
            m_new = jnp.maximum(m_old, jnp.max(logit, axis=-1, keepdims=True))
            p_ref[rows, :ts] = jnp.exp(logit - m_new).astype(BF16)
            al_ref[rows, :] = jnp.exp(m_old - m_new)
            m_ref[rows, :] = m_new
        acc_ref[...] = acc_ref[...] * al_ref[...] + _dot(p_ref[:, :ts], cb)

    n_past = q0 // KB

    def past_block(kb, carry):
        attend(pl.multiple_of(kb * KB, KB), KB, False)
        return carry

    lax.fori_loop(0, n_past, past_block, 0)

    def rem_block(r, carry):
        attend(pl.multiple_of(n_past * KB + r * tq, tq), tq, False)
        return carry

    lax.fori_loop(0, (q0 - n_past * KB) // tq, rem_block, 0)
    attend(pl.multiple_of(q0, tq), tq, True)

    y = jnp.zeros((tq, x_ref.shape[2]), F32)
    for pair in range(A_HEADS // 2):
        parts = []
        for h in (2 * pair, 2 * pair + 1):
            rows = slice(h * tq, (h + 1) * tq)
            parts.append((acc_ref[rows, :c] / acc_ref[rows, c + 4:c + 5]).astype(BF16))
        y = y + _dot(jnp.concatenate(parts, axis=1), wvo_ref[2 * pair * c:(2 * pair + 2) * c, :])
    o_ref[0] = x_ref[0] + gm_ref[0] * y


def _dsa_attn(x, gm, qabs, qidx, aux, kpad, caug, wvo):
    b, s, d = x.shape
    tq = min(TQ, s)
    hq = A_HEADS * tq
    row = pl.BlockSpec((1, 1, d), lambda bi, i: (bi, 0, 0))
    tile = lambda n: pl.BlockSpec((1, tq, n), lambda bi, i: (bi, i, 0))
    per_batch = lambda n: pl.BlockSpec((1, s, n), lambda bi, i: (bi, 0, 0))
    return pl.pallas_call(
        _dsa_attn_body,
        grid=(b, s // tq),
        in_specs=[tile(d), row, tile(qabs.shape[2]), tile(qidx.shape[2]), tile(LANES),
                  per_batch(LANES), per_batch(2 * A_KV_RANK), _full(wvo.shape)],
        out_specs=tile(d),
        out_shape=jax.ShapeDtypeStruct((b, s, d), F32),
        scratch_shapes=[pltpu.VMEM((s // LANES, tq, LANES), I32),
                        pltpu.VMEM((hq, 2 * A_KV_RANK), BF16),
                        pltpu.VMEM((IDX_HEADS * tq, LANES), BF16),
                        pltpu.VMEM((hq, tq), F32),
                        pltpu.VMEM((tq, LANES), I32),
                        pltpu.VMEM((hq, 1), F32),
                        pltpu.VMEM((hq, 1), F32),
                        pltpu.VMEM((hq, 2 * A_KV_RANK), F32),
                        pltpu.VMEM((hq, KB), F32),
                        pltpu.VMEM((hq, KB), BF16)],
        name="dsa_attn",
        compiler_params=_params("arbitrary", "arbitrary"),
    )(x, gm, qabs, qidx, aux, kpad, caug, wvo)


def _rwkv_proj_body(h_ref, sh_ref, sc_ref, gain_ref, mu_ref, wr_ref, wk_ref, wv_ref, w1_ref, w2_ref,
                    a1_ref, a2_ref, g1_ref, g2_ref, w0_ref, a0_ref,
                    r_ref, k_ref, v_ref, lw_ref, a_ref, g_ref, carry_ref):
    tm = h_ref.shape[1]

    @pl.when(pl.program_id(1) == 0)
    def _():
        carry_ref[...] = jnp.zeros_like(carry_ref)

    hn = _rms_mod(h_ref[0], gain_ref[...], sc_ref[0], sh_ref[0])
    prev = carry_ref[SUBLANES - 1:SUBLANES, :]
    carry_ref[...] = hn[tm - SUBLANES:, :]
    rid = lax.broadcasted_iota(I32, (tm, 1), 0)
    dx = jnp.where(rid == 0, prev, pltpu.roll(hn, 1, 0)) - hn
    mix = lambda j: (hn + dx * mu_ref[j:j + 1, :]).astype(BF16)
    r_ref[0] = _dot(mix(0), wr_ref[...])
    k_ref[0] = _dot(mix(2), wk_ref[...])
    v_ref[0] = _dot(mix(3), wv_ref[...])
    ww = w0_ref[...] + _dot(jnp.tanh(_dot(mix(1), w1_ref[...])).astype(BF16), w2_ref[...])
    softplus = jnp.maximum(-ww, 0.0) + jnp.log(1.0 + jnp.exp(-jnp.abs(ww)))
    lw_ref[0] = -jnp.exp(-softplus - 0.5)
    a_ref[0] = _sigmoid(a0_ref[...] + _dot(_dot(mix(4), a1_ref[...]).astype(BF16), a2_ref[...]))
    g_ref[0] = _dot(_sigmoid(_dot(mix(5), g1_ref[...])).astype(BF16), g2_ref[...])


def _pad_rank(w_down, w_up):
    r = w_down.shape[1]
    p = -r % LANES
    return (jnp.pad(w_down, ((0, 0), (0, p))).astype(BF16), jnp.pad(w_up, ((0, p), (0, 0))).astype(BF16))


def _rwkv_proj(h, sh, sc, gain, mu, w_r, w_k, w_v, w0, w1, w2, a0, a1, a2, g1, g2, tm=256):
    b, s, d = h.shape
    tm = min(tm, s)
    w1p, w2p = _pad_rank(w1, w2)
    a1p, a2p = _pad_rank(a1, a2)
    g1p, g2p = _pad_rank(g1, g2)
    row = pl.BlockSpec((1, 1, d), lambda bi, i: (bi, 0, 0))
    tile = pl.BlockSpec((1, tm, d), lambda bi, i: (bi, i, 0))
    big = [w.astype(BF16) for w in (w_r, w_k, w_v)]
    small = [w1p, w2p, a1p, a2p, g1p, g2p]
    out = jax.ShapeDtypeStruct((b, s, d), F32)
    return pl.pallas_call(
        _rwkv_proj_body,
        grid=(b, s // tm),
        in_specs=[tile, row, row, _full((1, d)), _full(mu.shape)] + [_full(w.shape) for w in big + small]
                 + [_full((1, d)), _full((1, d))],
        out_specs=[tile] * 6,
        out_shape=[out] * 6,
        scratch_shapes=[pltpu.VMEM((SUBLANES, d), F32)],
        name="rwkv_proj",
        compiler_params=_params("arbitrary", "arbitrary"),
    )(h, sh, sc, gain.reshape(1, d), mu, *big, *small, w0.reshape(1, d), a0.reshape(1, d))


def _rwkv_scan_body(n_chunks, r_ref, k_ref, v_ref, lw_ref, a_ref, kk_ref, ka_ref, rk_ref, gg_ref, gb_ref,
                    y_ref, st_ref):
    L = SCAN_L
    P2 = 2 * L

    @pl.when(pl.program_id(2) == 0)
    def _():
        st_ref[...] = jnp.zeros_like(st_ref)

    lane = lax.broadcasted_iota(I32, (1, LANES), 1)
    head0 = lane < B_HEAD
    m0 = jnp.where(head0, 1.0, 0.0)
    m1 = 1.0 - m0
    ri = lax.broadcasted_iota(I32, (P2, P2), 0)
    ci = lax.broadcasted_iota(I32, (P2, P2), 1)
    same = (ri >> 6) == (ci >> 6)
    strict = same & ((ci & 63) < (ri & 63))
    incl = same & ((ci & 63) <= (ri & 63))
    eye = jnp.where(ri == ci, 1.0, 0.0)
    tril = jnp.where(lax.broadcasted_iota(I32, (L, L), 1) <= lax.broadcasted_iota(I32, (L, L), 0),
                     1.0, 0.0).astype(BF16)
    k_k, k_a, r_k, gn_g, gn_b = kk_ref[...], ka_ref[...], rk_ref[...], gg_ref[...], gb_ref[...]

    def seg_sum(x):
        s0 = jnp.sum(x * m0, axis=-1, keepdims=True)
        s1 = jnp.sum(x * m1, axis=-1, keepdims=True)
        return jnp.where(head0, s0, s1)

    stack = lambda x: jnp.concatenate([x * m0, x * m1], axis=0)
    fold = lambda x: x[:L] + x[L:]

    def chunk(ci_, carry):
        rows = pl.ds(pl.multiple_of(ci_ * L, L), L)
        r, kraw, v, lw, asig = r_ref[0, rows, :], k_ref[0, rows, :], v_ref[0, rows, :], lw_ref[0, rows, :], a_ref[0, rows, :]
        kk = kraw * k_k
        kkn = kk / jnp.maximum(jnp.sqrt(seg_sum(kk * kk)), 1e-12)
        kmod = kraw * (1.0 + (asig - 1.0) * k_a)
        bvec = kkn * asig
        lw_hi, lw_lo = _split(lw)
        gc = _dot(tril, jnp.concatenate([lw_hi, lw_lo], axis=1))
        g = gc[:, :LANES] + gc[:, LANES:]
        g_last = g[L - 1:L, :]
        at = -kkn * jnp.exp(g - lw)
        rt = r * jnp.exp(g)
        inv = jnp.exp(-g)
        bh = bvec * inv
        kh = kmod * inv
        to_end = jnp.exp(g_last - g)
        bk_end = jnp.concatenate([bvec * to_end, kmod * to_end], axis=0)

        lhs = jnp.concatenate([stack(at), stack(rt)], axis=0).astype(BF16)
        rhs = jnp.concatenate([bh, bh, kh, kh], axis=0).astype(BF16)
        mm = _dot_nt(lhs, rhs)
        n_ab = jnp.where(strict, mm[:P2, :P2], 0.0)
        a_ak = jnp.where(strict, mm[:P2, P2:], 0.0)
        a_rb = jnp.where(incl, mm[P2:, :P2], 0.0)
        a_rk = jnp.where(incl, mm[P2:, P2:], 0.0)

        tmat = eye + n_ab
        pw = n_ab
        for _ in range(int(L).bit_length() - 2):
            pw = _dot3(pw, pw)
            tmat = tmat + _dot3(tmat, pw)

        sv = stack(v)
        akv = _dot3(jnp.concatenate([a_ak, a_rk], axis=0), sv)
        y0 = fold(akv[P2:])
        tw = _dot3(tmat, jnp.concatenate([stack(at), stack(fold(akv[:P2]))], axis=1))
        w_ = fold(tw[:, :LANES])
        u0 = fold(tw[:, LANES:])

        st = st_ref[...]
        ws = _dot3_nt(jnp.concatenate([w_, rt], axis=0), st)
        u = ws[:L] + u0
        y = ws[L:] + fold(_dot3(a_rb, stack(u))) + y0
        uv_t = jnp.concatenate([u, v], axis=0).T
        upd = _dot3(uv_t, bk_end)
        st_ref[...] = st * jnp.exp(g_last) + jnp.where(same, upd, 0.0)

        mu = seg_sum(y) * (1.0 / B_HEAD)
        dy = y - mu
        var = seg_sum(dy * dy) * (1.0 / B_HEAD)
        yn = dy * lax.rsqrt(var + GN_EPS) * gn_g + gn_b
        y_ref[0, rows, :] = yn + seg_sum(r * kmod * r_k) * v
        return carry

    lax.fori_loop(0, n_chunks, chunk, 0)


def _dot3_nt(a, b):
    ah, al = _split(a)
    bh, bl = _split(b)
    return _dot_nt(ah, bh) + (_dot_nt(ah, bl) + _dot_nt(al, bh))


def _rwkv_scan(r, k, v, lw, a, k_k, k_a, r_k, gn_g, gn_b, lb=512):
    b, s, d = r.shape
    lb = min(lb, s)
    tile = pl.BlockSpec((1, lb, LANES), lambda bi, p, i: (bi, i, p))
    prm = pl.BlockSpec((1, LANES), lambda bi, p, i: (0, p))
    vec = lambda x: x.reshape(1, d)
    return pl.pallas_call(
        functools.partial(_rwkv_scan_body, lb // SCAN_L),
        grid=(b, d // LANES, s // lb),
        in_specs=[tile] * 5 + [prm] * 5,
        out_specs=tile,
        out_shape=jax.ShapeDtypeStruct((b, s, d), F32),
        scratch_shapes=[pltpu.VMEM((LANES, LANES), F32)],
        name="rwkv_scan",
        compiler_params=_params("arbitrary", "arbitrary", "arbitrary"),
    )(r, k, v, lw, a, vec(k_k), vec(k_a), vec(r_k), vec(gn_g), vec(gn_b))


def _rwkv_out_body(h_ref, gm_ref, y_ref, g_ref, wo_ref, o_ref):
    o_ref[0] = h_ref[0] + gm_ref[0] * _dot((y_ref[0] * g_ref[0]).astype(BF16), wo_ref[...])


def _rwkv_out(h, gm, y, g, w_o, tm=512):
    b, s, d = h.shape
    tm = min(tm, s)
    row = pl.BlockSpec((1, 1, d), lambda bi, i: (bi, 0, 0))
    tile = pl.BlockSpec((1, tm, d), lambda bi, i: (bi, i, 0))
    return pl.pallas_call(
        _rwkv_out_body,
        grid=(b, s // tm),
        in_specs=[tile, row, tile, tile, _full((d, d))],
        out_specs=tile,
        out_shape=jax.ShapeDtypeStruct((b, s, d), F32),
        name="rwkv_out",
        compiler_params=_params("arbitrary", "arbitrary"),
    )(h, gm, y, g, w_o.astype(BF16))


def kernel(x, c, ada_w, ada_b, norm_mix, norm_ffn, norm_final, a_w_in, a_q_norm, a_kv_norm, a_kidx_g, a_kidx_b, a_w_uq, a_w_qi, a_w_uk, a_w_uv, a_w_o, b_mu, b_w_r, b_w_k, b_w_v, b_w_o, b_w0, b_w1, b_w2, b_a0, b_a1, b_a2, b_g1, b_g2, b_k_k, b_k_a, b_r_k, b_gn_g, b_gn_b, f_w_up, f_conv_w, f_conv_b, f_w_down):
    b, s, d = x.shape
    mod = _ada(c, ada_w, ada_b).reshape(ada_w.shape[0], b, 6, 1, d)
    sh_m, sc_m, g_m, sh_f, sc_f, g_f = (mod[:, :, j] for j in range(6))

    wq, wvo = _fold(a_w_uq[0], a_w_uk[0], a_w_uv[0], a_w_o[0])
    qabs, qidx, aux, kpad, caug = _dsa_proj(x, sh_m[0], sc_m[0], norm_mix[0], a_w_in[0], a_q_norm[0],
                                            a_kv_norm[0], a_kidx_g[0], a_kidx_b[0], wq, a_w_qi[0])
    h = _dsa_attn(x, g_m[0], qabs, qidx, aux, kpad, caug, wvo)
    h = _ffn(h, sh_f[0], sc_f[0], g_f[0], norm_ffn[0], f_w_up[0], f_conv_w[0], f_conv_b[0], f_w_down[0],
             norm_final, False)

    r, k, v, lw, a, g = _rwkv_proj(h, sh_m[1], sc_m[1], norm_mix[1], b_mu[0], b_w_r[0], b_w_k[0], b_w_v[0],
                                   b_w0[0], b_w1[0], b_w2[0], b_a0[0], b_a1[0], b_a2[0], b_g1[0], b_g2[0])
    y = _rwkv_scan(r, k, v, lw, a, b_k_k[0], b_k_a[0], b_r_k[0].reshape(-1), b_gn_g[0], b_gn_b[0])
    h = _rwkv_out(h, g_m[1], y, g, b_w_o[0])
    return _ffn(h, sh_f[1], sc_f[1], g_f[1], norm_ffn[1], f_w_up[1], f_conv_w[1], f_conv_b[1], f_w_down[1],
                norm_final, True)
```

```python
import functools

import numpy as np
import jax
import jax.numpy as jnp
from jax import lax
from jax.experimental import pallas as pl
from jax.experimental.pallas import tpu as pltpu

F32 = jnp.float32
BF16 = jnp.bfloat16
I32 = jnp.int32

EPS = 1e-6
CHUNK = 64
A_HEADS = 16
A_HEAD_DIM = 64
A_Q_RANK = 256
A_KV_RANK = 128
IDX_HEADS = 8
IDX_DIM = 64
TOPK_MAX = 256
B_HEAD = 64
GN_EPS = 64e-5
CONV_W = 3

LANES = 128
SUBLANES = 8
VMEM_LIMIT = 56 * 1024 * 1024
NEG = -1e30
INT_MIN = -(2 ** 31)

TQ = 128
KB = 512
SCAN_L = 64
NT = (((1,), (1,)), ((), ()))


def _dot(a, b):
    return jnp.dot(a, b, preferred_element_type=F32)


def _dot_nt(a, b):
    return lax.dot_general(a, b, NT, preferred_element_type=F32)


def _split(x):
    hi = x.astype(BF16)
    lo = (x - hi.astype(F32)).astype(BF16)
    return hi, lo


def _sigmoid(x):
    return 1.0 / (1.0 + jnp.exp(-x))


def _rms(x, gain):
    return x * lax.rsqrt(jnp.mean(x * x, axis=-1, keepdims=True) + EPS) * gain


def _rms_mod(x, gain, sc, sh):
    return _rms(x, gain) * (1.0 + sc) + sh


def _params(*sem):
    return pltpu.CompilerParams(dimension_semantics=sem, vmem_limit_bytes=VMEM_LIMIT)


def _full(shape):
    n = len(shape)
    return pl.BlockSpec(shape, lambda *_: (0,) * n)


def _ada_body(c_ref, w_ref, b_ref, o_ref):
    c = c_ref[...]
    o_ref[0] = _dot(c * _sigmoid(c), w_ref[0]) + b_ref[0]


def _ada(c, ada_w, ada_b):
    depth, d, n = ada_w.shape
    b = c.shape[0]
    tn = n // 4
    return pl.pallas_call(
        _ada_body,
        grid=(depth, n // tn),
        in_specs=[pl.BlockSpec((b, d), lambda i, j: (0, 0)),
                  pl.BlockSpec((1, d, tn), lambda i, j: (i, 0, j)),
                  pl.BlockSpec((1, 1, tn), lambda i, j: (i, 0, j))],
        out_specs=pl.BlockSpec((1, b, tn), lambda i, j: (i, 0, j)),
        out_shape=jax.ShapeDtypeStruct((depth, b, n), F32),
        name="ada_mod",
        compiler_params=_params("arbitrary", "arbitrary"),
    )(c, ada_w, ada_b.reshape(depth, 1, n))


def _ffn_body(final_norm, nf, h_ref, sh_ref, sc_ref, g_ref, gain_ref, wup_ref, cw_ref, cb_ref,
              wdn_ref, gfin_ref, o_ref, carry_ref, hn_ref, acc_ref):
    tm = h_ref.shape[1]

    @pl.when(pl.program_id(1) == 0)
    def _():
        carry_ref[...] = jnp.zeros_like(carry_ref)

    hn_ref[...] = _rms_mod(h_ref[0], gain_ref[...], sc_ref[0], sh_ref[0]).astype(BF16)
    acc_ref[...] = jnp.zeros_like(acc_ref)
    rid = lax.broadcasted_iota(I32, (tm, 1), 0)

    def conv(part, j):
        u = _dot(hn_ref[...], wup_ref[part, j])
        prev = carry_ref[part, j]
        carry_ref[part, j] = u[tm - SUBLANES:, :]
        p1 = prev[SUBLANES - 1:SUBLANES, :]
        p2 = prev[SUBLANES - 2:SUBLANES - 1, :]
        s1 = jnp.where(rid == 0, p1, pltpu.roll(u, 1, 0))
        s2 = jnp.where(rid == 0, p2, jnp.where(rid == 1, p1, pltpu.roll(u, 2, 0)))
        cw = cw_ref[part, j]
        return cw[2:3] * u + cw[1:2] * s1 + cw[0:1] * s2 + cb_ref[part, j]

    def chunk(j, carry):
        ug = conv(0, j)
        uv = conv(1, j)
        act = (ug * _sigmoid(ug) * uv).astype(BF16)
        acc_ref[...] += _dot(act, wdn_ref[j])
        return carry

    lax.fori_loop(0, nf, chunk, 0)
    out = h_ref[0] + g_ref[0] * acc_ref[...]
    if final_norm:
        out = _rms(out, gfin_ref[...])
    o_ref[0] = out


def _ffn(h, sh, sc, g, gain, w_up, conv_w, conv_b, w_down, gfin, final_norm, tm=512, tf=256):
    b, s, d = h.shape
    f = w_down.shape[0]
    nf = f // tf
    tm = min(tm, s)
    wup = w_up.astype(BF16).reshape(d, 2, nf, tf).transpose(1, 2, 0, 3)
    cw = conv_w.reshape(CONV_W, 2, nf, tf).transpose(1, 2, 0, 3)
    cb = conv_b.reshape(2, nf, 1, tf)
    wdn = w_down.astype(BF16).reshape(nf, tf, d)
    row = pl.BlockSpec((1, 1, d), lambda bi, i: (bi, 0, 0))
    tile = pl.BlockSpec((1, tm, d), lambda bi, i: (bi, i, 0))
    return pl.pallas_call(
        functools.partial(_ffn_body, final_norm, nf),
        grid=(b, s // tm),
        in_specs=[tile, row, row, row, _full((1, d)), _full(wup.shape), _full(cw.shape),
                  _full(cb.shape), _full(wdn.shape), _full((1, d))],
        out_specs=tile,
        out_shape=jax.ShapeDtypeStruct((b, s, d), F32),
        scratch_shapes=[pltpu.VMEM((2, nf, SUBLANES, tf), F32),
                        pltpu.VMEM((tm, d), BF16),
                        pltpu.VMEM((tm, d), F32)],
        name="conv_ffn",
        compiler_params=_params("arbitrary", "arbitrary"),
    )(h, sh, sc, g, gain.reshape(1, d), wup, cw, cb, wdn, gfin.reshape(1, d))


def _fold_body(wuq_ref, wuk_ref, wuv_ref, wo_ref, wq_ref, wvo_ref):
    hi = lax.Precision.HIGHEST
    wq_ref[0] = lax.dot_general(wuq_ref[0], wuk_ref[0], NT, precision=hi,
                                preferred_element_type=F32) * (A_HEAD_DIM ** -0.5)
    wvo_ref[0] = jnp.dot(wuv_ref[0], wo_ref[0], precision=hi, preferred_element_type=F32)


def _fold(w_uq, w_uk, w_uv, w_o):
    rq = w_uq.shape[0]
    h, c, dh = w_uk.shape
    d = w_o.shape[1]
    wuq = w_uq.reshape(rq, h, dh).transpose(1, 0, 2)
    wo = w_o.reshape(h, dh, d)
    per_head = lambda *shape: pl.BlockSpec((1,) + shape, lambda i: (i, 0, 0))
    wq, wvo = pl.pallas_call(
        _fold_body,
        grid=(h,),
        in_specs=[per_head(rq, dh), per_head(c, dh), per_head(c, dh), per_head(dh, d)],
        out_specs=[per_head(rq, c), per_head(c, d)],
        out_shape=[jax.ShapeDtypeStruct((h, rq, c), F32), jax.ShapeDtypeStruct((h, c, d), F32)],
        name="dsa_fold",
        compiler_params=_params("arbitrary"),
    )(wuq, w_uk, w_uv, wo)
    return (wq.transpose(1, 0, 2).reshape(rq, h * c).astype(BF16),
            wvo.reshape(h * c, d).astype(BF16))


def _dsa_proj_body(x_ref, sh_ref, sc_ref, gain_ref, win_ref, qn_ref, kvn_ref, kg_ref, kb_ref,
                   wq_ref, wqi_ref, qabs_ref, qidx_ref, aux_ref, kpad_ref, caug_ref):
    tm = x_ref.shape[1]
    hn = _rms_mod(x_ref[0], gain_ref[...], sc_ref[0], sh_ref[0]).astype(BF16)
    proj = _dot(hn, win_ref[...])
    ql = _rms(proj[:, :A_Q_RANK], qn_ref[...]).astype(BF16)
    ck = _rms(proj[:, A_Q_RANK:A_Q_RANK + A_KV_RANK], kvn_ref[...])
    slab = proj[:, A_Q_RANK + A_KV_RANK:]
    lane = lax.broadcasted_iota(I32, (1, LANES), 1)
    is_k = lane < IDX_DIM
    mu = jnp.sum(jnp.where(is_k, slab, 0.0), axis=-1, keepdims=True) * (1.0 / IDX_DIM)
    dk = jnp.where(is_k, slab - mu, 0.0)
    var = jnp.sum(dk * dk, axis=-1, keepdims=True) * (1.0 / IDX_DIM)
    kpad_ref[0] = (dk * lax.rsqrt(var + EPS) * kg_ref[...] + kb_ref[...]).astype(BF16)
    aux_ref[0] = slab * (IDX_HEADS ** -0.5 * IDX_DIM ** -0.5)
    qabs_ref[0] = _dot(ql, wq_ref[...]).astype(BF16)
    qidx_ref[0] = _dot(ql, wqi_ref[...]).astype(BF16)
    pos = pl.program_id(1) * tm + lax.broadcasted_iota(I32, (tm, 1), 0)
    p_hi = (pos >> 6).astype(F32)
    p_lo = (pos & 63).astype(F32)
    extra = jnp.where(lane < 2, p_hi, jnp.where(lane < 4, p_lo, jnp.where(lane == 4, 1.0, 0.0)))
    caug_ref[0, :, :A_KV_RANK] = ck.astype(BF16)
    caug_ref[0, :, A_KV_RANK:] = extra.astype(BF16)


def _dsa_proj(x, sh, sc, gain, w_in, q_norm, kv_norm, kidx_g, kidx_b, wq, w_qi, tm=512):
    b, s, d = x.shape
    tm = min(tm, s)
    n_in = w_in.shape[1]
    win = jnp.pad(w_in, ((0, 0), (0, 4 * LANES - n_in))).astype(BF16)
    wqi = jnp.pad(w_qi.reshape(A_Q_RANK, IDX_HEADS, IDX_DIM),
                  ((0, 0), (0, 0), (0, LANES - IDX_DIM))).reshape(A_Q_RANK, IDX_HEADS * LANES).astype(BF16)
    kg = jnp.pad(kidx_g, (0, LANES - IDX_DIM)).reshape(1, LANES)
    kb = jnp.pad(kidx_b, (0, LANES - IDX_DIM)).reshape(1, LANES)
    row = pl.BlockSpec((1, 1, d), lambda bi, i: (bi, 0, 0))
    tile = lambda n: pl.BlockSpec((1, tm, n), lambda bi, i: (bi, i, 0))
    nq, ni = A_HEADS * A_KV_RANK, IDX_HEADS * LANES
    return pl.pallas_call(
        _dsa_proj_body,
        grid=(b, s // tm),
        in_specs=[tile(d), row, row, _full((1, d)), _full(win.shape), _full((1, A_Q_RANK)),
                  _full((1, A_KV_RANK)), _full((1, LANES)), _full((1, LANES)),
                  _full(wq.shape), _full(wqi.shape)],
        out_specs=[tile(nq), tile(ni), tile(LANES), tile(LANES), tile(2 * A_KV_RANK)],
        out_shape=[jax.ShapeDtypeStruct((b, s, nq), BF16), jax.ShapeDtypeStruct((b, s, ni), BF16),
                   jax.ShapeDtypeStruct((b, s, LANES), F32), jax.ShapeDtypeStruct((b, s, LANES), BF16),
                   jax.ShapeDtypeStruct((b, s, 2 * A_KV_RANK), BF16)],
        name="dsa_proj",
        compiler_params=_params("arbitrary", "arbitrary"),
    )(x, sh, sc, gain.reshape(1, d), win, q_norm.reshape(1, -1), kv_norm.reshape(1, -1), kg, kb, wq, wqi)


def _bf16_digits(v):
    hi = float(np.asarray(v, np.float32).astype(BF16).astype(np.float32))
    lo = float(np.asarray(v - hi, np.float32).astype(BF16).astype(np.float32))
    return hi, lo


def _dsa_attn_body(x_ref, gm_ref, qabs_ref, qidx_ref, aux_ref, kpad_ref, caug_ref, wvo_ref, o_ref,
                   keys_ref, qaug_ref, qis_ref, tab_ref, t_ref, m_ref, al_ref, acc_ref, s_ref, p_ref):
    tq = x_ref.shape[1]
    seq = kpad_ref.shape[1]
    c = A_KV_RANK
    q0 = pl.program_id(1) * tq
    kend = q0 + tq
    n_kb = (kend + KB - 1) // KB
    n128 = n_kb * (KB // LANES)
    lane = lax.broadcasted_iota(I32, (1, LANES), 1)
    trow = lax.broadcasted_iota(I32, (tq, 1), 0)
    slopes = [2.0 ** (-8.0 * (h + 1) / A_HEADS) for h in range(A_HEADS)]

    rel = jnp.maximum(lax.broadcasted_iota(I32, (tq, tq), 1) - trow, 0).astype(F32)
    adm_d = (lax.broadcasted_iota(I32, (tq, tq), 1) >> 6) <= (trow >> 6)
    for h in range(A_HEADS):
        rows = slice(h * tq, (h + 1) * tq)
        hi, lo = _bf16_digits(slopes[h])
        digits = jnp.where(lane == 0, 64.0 * hi, jnp.where(lane == 1, 64.0 * lo,
                           jnp.where(lane == 2, hi, jnp.where(lane == 3, lo, 0.0))))
        qaug_ref[rows, :c] = qabs_ref[0, :, h * c:(h + 1) * c]
        qaug_ref[rows, c:] = jnp.broadcast_to(digits, (tq, LANES)).astype(BF16)
        tab_ref[rows, :] = jnp.where(adm_d, (-2.0 * slopes[h]) * rel, NEG)
    for h in range(IDX_HEADS):
        qis_ref[h * tq:(h + 1) * tq, :] = qidx_ref[0, :, h * LANES:(h + 1) * LANES]

    qchunk = (q0 + trow) >> 6
    aux = aux_ref[0]

    def score_block(kb, carry):
        off = pl.multiple_of(kb * KB, KB)
        z = _dot_nt(qis_ref[...], kpad_ref[0, pl.ds(off, KB), :])
        sc = jnp.zeros((tq, KB), F32)
        for h in range(IDX_HEADS):
            w = aux[:, IDX_DIM + h:IDX_DIM + h + 1]
            sc = sc + w * jnp.maximum(z[h * tq:(h + 1) * tq, :], 0.0)
        kchunk = (off + lax.broadcasted_iota(I32, (1, KB), 1)) >> 6
        sc = jnp.where(kchunk <= qchunk, sc, -jnp.inf)
        bits = pltpu.bitcast(sc, I32)
        key = bits ^ ((bits >> 31) & 0x7FFFFFFF)
        for j in range(KB // LANES):
            keys_ref[kb * (KB // LANES) + j] = key[:, j * LANES:(j + 1) * LANES]
        return carry

    lax.fori_loop(0, n_kb, score_block, 0)

    def count(pred):
        def body(j, acc):
            return acc + jnp.where(pred(keys_ref[j], j), 1.0, 0.0)
        part = lax.fori_loop(0, n128, body, jnp.zeros((tq, LANES), F32))
        return jnp.sum(part, axis=-1, keepdims=True)

    t_ref[...] = jnp.full(t_ref.shape, INT_MIN, I32)

    @pl.when(kend > TOPK_MAX)
    def _():
        kf = float(TOPK_MAX)
        n_pos = count(lambda k, j: k >= 0)
        t0 = jnp.where(n_pos >= kf, 0, INT_MIN).astype(I32)

        def bit_step(i, t):
            cand = t | jnp.left_shift(jnp.int32(1), 30 - i)
            return jnp.where(count(lambda k, j: k >= cand) >= kf, cand, t)

        t = lax.fori_loop(0, 31, bit_step, t0)
        t_ref[...] = jnp.broadcast_to(t, t_ref.shape)
        n_ge = count(lambda k, j: k >= t)

        @pl.when(jnp.max(n_ge) > kf)
        def _():
            need = kf - count(lambda k, j: k > t)
            col = lambda j: j * LANES + lane
            n_bits = max(1, int(seq - 1).bit_length())

            def idx_step(i, x):
                cand = x | jnp.left_shift(jnp.int32(1), n_bits - 1 - i)
                below = count(lambda k, j: (k == t) & (col(j) < cand))
                return jnp.where(below < need, cand, x)

            last = lax.fori_loop(0, n_bits, idx_step, jnp.zeros((tq, 1), I32))
            demoted = jnp.where(t == INT_MIN, t, t - 1)

            def demote(j, carry):
                k = keys_ref[j]
                keys_ref[j] = jnp.where((k == t) & (col(j) > last), demoted, k)
                return carry

            lax.fori_loop(0, n128, demote, 0)

    m_ref[...] = jnp.full(m_ref.shape, NEG, F32)
    acc_ref[...] = jnp.zeros_like(acc_ref)
    thr = t_ref[:, 0:1]

    def attend(off, ts, diag):
        cb = caug_ref[0, pl.ds(off, ts), :]
        s_ref[:, :ts] = _dot_nt(qaug_ref[...], cb)
        j0 = off // LANES
        key = jnp.concatenate([keys_ref[j0 + j] for j in range(ts // LANES)], axis=1)
        bias = jnp.where(key >= thr, 0.0, NEG)
        for h in range(A_HEADS):
            rows = slice(h * tq, (h + 1) * tq)
            logit = s_ref[rows, :ts] + bias
            if diag:
                logit = logit + tab_ref[rows, :]
            m_old = m_ref[rows, :]
            m_new = jnp.maximum(m_old, jnp.max(logit, axis=-1, keepdims=True))
            p_ref[rows, :ts] = jnp.exp(logit - m_new).astype(BF16)
            al_ref[rows, :] = jnp.exp(m_old - m_new)
            m_ref[rows, :] = m_new
        acc_ref[...] = acc_ref[...] * al_ref[...] + _dot(p_ref[:, :ts], cb)

    n_past = q0 // KB

    def past_block(kb, carry):
        attend(pl.multiple_of(kb * KB, KB), KB, False)
        return carry

    lax.fori_loop(0, n_past, past_block, 0)

    def rem_block(r, carry):
        attend(pl.multiple_of(n_past * KB + r * tq, tq), tq, False)
        return carry

    lax.fori_loop(0, (q0 - n_past * KB) // tq, rem_block, 0)
    attend(pl.multiple_of(q0, tq), tq, True)

    y = jnp.zeros((tq, x_ref.shape[2]), F32)
    for pair in range(A_HEADS // 2):
        parts = []
        for h in (2 * pair, 2 * pair + 1):
            rows = slice(h * tq, (h + 1) * tq)
            parts.append((acc_ref[rows, :c] / acc_ref[rows, c + 4:c + 5]).astype(BF16))
        y = y + _dot(jnp.concatenate(parts, axis=1), wvo_ref[2 * pair * c:(2 * pair + 2) * c, :])
    o_ref[0] = x_ref[0] + gm_ref[0] * y


def _dsa_attn(x, gm, qabs, qidx, aux, kpad, caug, wvo):
    b, s, d = x.shape
    tq = min(TQ, s)
    hq = A_HEADS * tq
    row = pl.BlockSpec((1, 1, d), lambda bi, i: (bi, 0, 0))
    tile = lambda n: pl.BlockSpec((1, tq, n), lambda bi, i: (bi, i, 0))
    per_batch = lambda n: pl.BlockSpec((1, s, n), lambda bi, i: (bi, 0, 0))
    return pl.pallas_call(
        _dsa_attn_body,
        grid=(b, s // tq),
        in_specs=[tile(d), row, tile(qabs.shape[2]), tile(qidx.shape[2]), tile(LANES),
                  per_batch(LANES), per_batch(2 * A_KV_RANK), _full(wvo.shape)],
        out_specs=tile(d),
        out_shape=jax.ShapeDtypeStruct((b, s, d), F32),
        scratch_shapes=[pltpu.VMEM((s // LANES, tq, LANES), I32),
                        pltpu.VMEM((hq, 2 * A_KV_RANK), BF16),
                        pltpu.VMEM((IDX_HEADS * tq, LANES), BF16),
                        pltpu.VMEM((hq, tq), F32),
                        pltpu.VMEM((tq, LANES), I32),
                        pltpu.VMEM((hq, 1), F32),
                        pltpu.VMEM((hq, 1), F32),
                        pltpu.VMEM((hq, 2 * A_KV_RANK), F32),
                        pltpu.VMEM((hq, KB), F32),
                        pltpu.VMEM((hq, KB), BF16)],
        name="dsa_attn",
        compiler_params=_params("arbitrary", "arbitrary"),
    )(x, gm, qabs, qidx, aux, kpad, caug, wvo)


def _rwkv_proj_body(h_ref, sh_ref, sc_ref, gain_ref, mu_ref, wr_ref, wk_ref, wv_ref, w1_ref, w2_ref,
                    a1_ref, a2_ref, g1_ref, g2_ref, w0_ref, a0_ref,
                    r_ref, k_ref, v_ref, lw_ref, a_ref, g_ref, carry_ref):
    tm = h_ref.shape[1]

    @pl.when(pl.program_id(1) == 0)
    def _():
        carry_ref[...] = jnp.zeros_like(carry_ref)

    hn = _rms_mod(h_ref[0], gain_ref[...], sc_ref[0], sh_ref[0])
    prev = carry_ref[SUBLANES - 1:SUBLANES, :]
    carry_ref[...] = hn[tm - SUBLANES:, :]
    rid = lax.broadcasted_iota(I32, (tm, 1), 0)
    dx = jnp.where(rid == 0, prev, pltpu.roll(hn, 1, 0)) - hn
    mix = lambda j: (hn + dx * mu_ref[j:j + 1, :]).astype(BF16)

    def put(ref, val):
        for p in range(val.shape[1] // LANES):
            ref[0, p] = val[:, p * LANES:(p + 1) * LANES]

    put(r_ref, _dot(mix(0), wr_ref[...]))
    put(k_ref, _dot(mix(2), wk_ref[...]))
    put(v_ref, _dot(mix(3), wv_ref[...]))
    ww = w0_ref[...] + _dot(jnp.tanh(_dot(mix(1), w1_ref[...])).astype(BF16), w2_ref[...])
    softplus = jnp.maximum(-ww, 0.0) + jnp.log(1.0 + jnp.exp(-jnp.abs(ww)))
    put(lw_ref, -jnp.exp(-softplus - 0.5))
    put(a_ref, _sigmoid(a0_ref[...] + _dot(_dot(mix(4), a1_ref[...]).astype(BF16), a2_ref[...])))
    g_ref[0] = _dot(_sigmoid(_dot(mix(5), g1_ref[...])).astype(BF16), g2_ref[...])


def _pad_rank(w_down, w_up):
    r = w_down.shape[1]
    p = -r % LANES
    return (jnp.pad(w_down, ((0, 0), (0, p))).astype(BF16), jnp.pad(w_up, ((0, p), (0, 0))).astype(BF16))


def _rwkv_proj(h, sh, sc, gain, mu, w_r, w_k, w_v, w0, w1, w2, a0, a1, a2, g1, g2, tm=256):
    b, s, d = h.shape
    tm = min(tm, s)
    w1p, w2p = _pad_rank(w1, w2)
    a1p, a2p = _pad_rank(a1, a2)
    g1p, g2p = _pad_rank(g1, g2)
    row = pl.BlockSpec((1, 1, d), lambda bi, i: (bi, 0, 0))
    tile = pl.BlockSpec((1, tm, d), lambda bi, i: (bi, i, 0))
    big = [w.astype(BF16) for w in (w_r, w_k, w_v)]
    small = [w1p, w2p, a1p, a2p, g1p, g2p]
    np_ = d // LANES
    ptile = pl.BlockSpec((1, np_, tm, LANES), lambda bi, i: (bi, 0, i, 0))
    pout = jax.ShapeDtypeStruct((b, np_, s, LANES), F32)
    return pl.pallas_call(
        _rwkv_proj_body,
        grid=(b, s // tm),
        in_specs=[tile, row, row, _full((1, d)), _full(mu.shape)] + [_full(w.shape) for w in big + small]
                 + [_full((1, d)), _full((1, d))],
        out_specs=[ptile] * 5 + [tile],
        out_shape=[pout] * 5 + [jax.ShapeDtypeStruct((b, s, d), F32)],
        scratch_shapes=[pltpu.VMEM((SUBLANES, d), F32)],
        name="rwkv_proj",
        compiler_params=_params("arbitrary", "arbitrary"),
    )(h, sh, sc, gain.reshape(1, d), mu, *big, *small, w0.reshape(1, d), a0.reshape(1, d))


def _rwkv_scan_body(n_chunks, r_ref, k_ref, v_ref, lw_ref, a_ref, kk_ref, ka_ref, rk_ref, gg_ref, gb_ref,
                    y_ref, st_ref, w_s, rt_s, u0_s, y0_s, bon_s, arb_s, bkt_s, plt_s):
    L = SCAN_L
    P2 = 2 * L
    n_pairs = r_ref.shape[1]

    @pl.when(pl.program_id(1) == 0)
    def _():
        st_ref[...] = jnp.zeros_like(st_ref)

    lane = lax.broadcasted_iota(I32, (1, LANES), 1)
    head0 = lane < B_HEAD
    m0 = jnp.where(head0, 1.0, 0.0)
    m1 = 1.0 - m0
    ri = lax.broadcasted_iota(I32, (P2, P2), 0)
    ci = lax.broadcasted_iota(I32, (P2, P2), 1)
    same = (ri >> 6) == (ci >> 6)
    strict = same & ((ci & 63) < (ri & 63))
    incl = same & ((ci & 63) <= (ri & 63))
    eye = jnp.where(ri == ci, 1.0, 0.0)
    tril = jnp.where(lax.broadcasted_iota(I32, (L, L), 1) <= lax.broadcasted_iota(I32, (L, L), 0),
                     1.0, 0.0).astype(BF16)
    def seg_sum(x):
        s0 = jnp.sum(x * m0, axis=-1, keepdims=True)
        s1 = jnp.sum(x * m1, axis=-1, keepdims=True)
        return jnp.where(head0, s0, s1)

    stack = lambda x: jnp.concatenate([x * m0, x * m1], axis=0)
    fold = lambda x: x[:L] + x[L:]

    bdot = lambda a, b: _dot(a.astype(BF16), b.astype(BF16))

    each = lambda f, *xs: [f(*a) for a in zip(*xs)]

    def prepare_pair(p, carry):
        k_k, k_a, r_k = kk_ref[p], ka_ref[p], rk_ref[p]
        cs = range(n_chunks)
        r, kraw, v, lw, asig = ([ref[0, p, c * L:(c + 1) * L, :] for c in cs]
                                for ref in (r_ref, k_ref, v_ref, lw_ref, a_ref))
        kk = each(lambda x: x * k_k, kraw)
        kkn = each(lambda x: x / jnp.maximum(jnp.sqrt(seg_sum(x * x)), 1e-12), kk)
        kmod = each(lambda x, a: x * (1.0 + (a - 1.0) * k_a), kraw, asig)
        bvec = each(jnp.multiply, kkn, asig)
        gc = each(lambda x: _dot(tril, jnp.concatenate(_split(x), axis=1)), lw)
        g = each(lambda x: x[:, :LANES] + x[:, LANES:], gc)
        g_last = each(lambda x: x[L - 1:L, :], g)
        at = each(lambda kn, x, w: -kn * jnp.exp(x - w), kkn, g, lw)
        rt = each(lambda x, y: x * jnp.exp(y), r, g)
        inv = each(lambda x: jnp.exp(-x), g)
        to_end = each(lambda x, y: jnp.exp(x - y), g_last, g)
        bk_end = each(lambda b_, k_, e: jnp.concatenate([b_ * e, k_ * e], axis=0), bvec, kmod, to_end)
        lhs = each(lambda a, b_: jnp.concatenate([stack(a), stack(b_)], axis=0).astype(BF16), at, rt)
        rhs = each(lambda b_, k_, i: jnp.concatenate([b_ * i, b_ * i, k_ * i, k_ * i], axis=0).astype(BF16),
                   bvec, kmod, inv)
        mm = each(_dot_nt, lhs, rhs)
        n_ab = each(lambda x: jnp.where(strict, x[:P2, :P2], 0.0), mm)
        a_kk = each(lambda x: jnp.concatenate([jnp.where(strict, x[:P2, P2:], 0.0),
                                               jnp.where(incl, x[P2:, P2:], 0.0)], axis=0), mm)
        a_rb = each(lambda x: jnp.where(incl, x[P2:, :P2], 0.0), mm)

        tmat = each(lambda x: eye + x, n_ab)
        pw = n_ab
        for _ in range(int(L).bit_length() - 2):
            pw = each(bdot, pw, pw)
            tmat = each(lambda t, q: t + bdot(t, q), tmat, pw)

        akv = each(lambda a, x: bdot(a, stack(x)), a_kk, v)
        tw = each(lambda t, a, x: bdot(t, jnp.concatenate([stack(a), stack(fold(x[:P2]))], axis=1)),
                  tmat, at, akv)
        for c in cs:
            w_s[p, c] = fold(tw[c][:, :LANES]).astype(BF16)
            rt_s[p, c] = rt[c].astype(BF16)
            u0_s[p, c] = fold(tw[c][:, LANES:])
            y0_s[p, c] = fold(akv[c][P2:])
            bon_s[p, c] = seg_sum(r[c] * kmod[c] * r_k) * v[c]
            arb_s[p, c] = a_rb[c].astype(BF16)
            bkt_s[p, c] = bk_end[c].T.astype(BF16)
            plt_s[p, c] = jnp.broadcast_to(jnp.exp(g_last[c]), (P2, LANES)).T
        return carry

    lax.fori_loop(0, n_pairs, prepare_pair, 0)

    def advance(c, carry):
        rows = pl.ds(pl.multiple_of(c * L, L), L)
        ps = range(n_pairs)
        st = [st_ref[p] for p in ps]
        ws = [_dot(jnp.concatenate([w_s[p, c], rt_s[p, c]], axis=0), st[p].astype(BF16)) for p in ps]
        u = [ws[p][:L] + u0_s[p, c] for p in ps]
        uv = [jnp.concatenate([u[p], v_ref[0, p, rows, :]], axis=0).astype(BF16) for p in ps]
        upd = [_dot(bkt_s[p, c], uv[p]) for p in ps]
        for p in ps:
            st_ref[p] = st[p] * plt_s[p, c] + jnp.where(same, upd[p], 0.0)
        au = [_dot(arb_s[p, c], stack(u[p]).astype(BF16)) for p in ps]
        for p in ps:
            y = ws[p][L:] + fold(au[p]) + y0_s[p, c]
            mu = seg_sum(y) * (1.0 / B_HEAD)
            dy = y - mu
            var = seg_sum(dy * dy) * (1.0 / B_HEAD)
            y_ref[0, p, rows, :] = dy * lax.rsqrt(var + GN_EPS) * gg_ref[p] + gb_ref[p] + bon_s[p, c]
        return carry

    lax.fori_loop(0, n_chunks, advance, 0)


def _rwkv_scan(r, k, v, lw, a, k_k, k_a, r_k, gn_g, gn_b, lb=512):
    b, n_pairs, s, _ = r.shape
    lb = min(lb, s)
    nc = lb // SCAN_L
    tile = pl.BlockSpec((1, n_pairs, lb, LANES), lambda bi, i: (bi, 0, i, 0))
    vec = lambda x: x.reshape(n_pairs, 1, LANES)
    per_chunk = lambda rows, dt: pltpu.VMEM((n_pairs, nc, rows, LANES), dt)
    return pl.pallas_call(
        functools.partial(_rwkv_scan_body, nc),
        grid=(b, s // lb),
        in_specs=[tile] * 5 + [_full((n_pairs, 1, LANES))] * 5,
        out_specs=tile,
        out_shape=jax.ShapeDtypeStruct(r.shape, F32),
        scratch_shapes=[pltpu.VMEM((n_pairs, LANES, LANES), F32),
                        per_chunk(SCAN_L, BF16), per_chunk(SCAN_L, BF16),
                        per_chunk(SCAN_L, F32), per_chunk(SCAN_L, F32), per_chunk(SCAN_L, F32),
                        per_chunk(2 * SCAN_L, BF16), per_chunk(2 * SCAN_L, BF16),
                        per_chunk(2 * SCAN_L, F32)],
        name="rwkv_scan",
        compiler_params=_params("arbitrary", "arbitrary"),
    )(r, k, v, lw, a, vec(k_k), vec(k_a), vec(r_k), vec(gn_g), vec(gn_b))


def _rwkv_out_body(h_ref, gm_ref, y_ref, g_ref, wo_ref, o_ref):
    y = jnp.concatenate([y_ref[0, p] for p in range(y_ref.shape[1])], axis=1)
    o_ref[0] = h_ref[0] + gm_ref[0] * _dot((y * g_ref[0]).astype(BF16), wo_ref[...])


def _rwkv_out(h, gm, y, g, w_o, tm=512):
    b, s, d = h.shape
    tm = min(tm, s)
    row = pl.BlockSpec((1, 1, d), lambda bi, i: (bi, 0, 0))
    tile = pl.BlockSpec((1, tm, d), lambda bi, i: (bi, i, 0))
    ptile = pl.BlockSpec((1, d // LANES, tm, LANES), lambda bi, i: (bi, 0, i, 0))
    return pl.pallas_call(
        _rwkv_out_body,
        grid=(b, s // tm),
        in_specs=[tile, row, ptile, tile, _full((d, d))],
        out_specs=tile,
        out_shape=jax.ShapeDtypeStruct((b, s, d), F32),
        name="rwkv_out",
        compiler_params=_params("arbitrary", "arbitrary"),
    )(h, gm, y, g, w_o.astype(BF16))


def kernel(x, c, ada_w, ada_b, norm_mix, norm_ffn, norm_final, a_w_in, a_q_norm, a_kv_norm, a_kidx_g, a_kidx_b, a_w_uq, a_w_qi, a_w_uk, a_w_uv, a_w_o, b_mu, b_w_r, b_w_k, b_w_v, b_w_o, b_w0, b_w1, b_w2, b_a0, b_a1, b_a2, b_g1, b_g2, b_k_k, b_k_a, b_r_k, b_gn_g, b_gn_b, f_w_up, f_conv_w, f_conv_b, f_w_down):
    b, s, d = x.shape
    mod = _ada(c, ada_w, ada_b).reshape(ada_w.shape[0], b, 6, 1, d)
    sh_m, sc_m, g_m, sh_f, sc_f, g_f = (mod[:, :, j] for j in range(6))

    wq, wvo = _fold(a_w_uq[0], a_w_uk[0], a_w_uv[0], a_w_o[0])
    qabs, qidx, aux, kpad, caug = _dsa_proj(x, sh_m[0], sc_m[0], norm_mix[0], a_w_in[0], a_q_norm[0],
                                            a_kv_norm[0], a_kidx_g[0], a_kidx_b[0], wq, a_w_qi[0])
    h = _dsa_attn(x, g_m[0], qabs, qidx, aux, kpad, caug, wvo)
    h = _ffn(h, sh_f[0], sc_f[0], g_f[0], norm_ffn[0], f_w_up[0], f_conv_w[0], f_conv_b[0], f_w_down[0],
             norm_final, False)

    r, k, v, lw, a, g = _rwkv_proj(h, sh_m[1], sc_m[1], norm_mix[1], b_mu[0], b_w_r[0], b_w_k[0], b_w_v[0],
                                   b_w0[0], b_w1[0], b_w2[0], b_a0[0], b_a1[0], b_a2[0], b_g1[0], b_g2[0])
    y = _rwkv_scan(r, k, v, lw, a, b_k_k[0], b_k_a[0], b_r_k[0].reshape(-1), b_gn_g[0], b_gn_b[0])
    h = _rwkv_out(h, g_m[1], y, g, b_w_o[0])
    return _ffn(h, sh_f[1], sc_f[1], g_f[1], norm_ffn[1], f_w_up[1], f_conv_w[1], f_conv_b[1], f_w_down[1],
                norm_final, True)
```

```python
import functools

import numpy as np
import jax
import jax.numpy as jnp
from jax import lax
from jax.experimental import pallas as pl
from jax.experimental.pallas import tpu as pltpu

F32 = jnp.float32
BF16 = jnp.bfloat16
I32 = jnp.int32

EPS = 1e-6
CHUNK = 64
A_HEADS = 16
A_HEAD_DIM = 64
A_Q_RANK = 256
A_KV_RANK = 128
IDX_HEADS = 8
IDX_DIM = 64
TOPK_MAX = 256
B_HEAD = 64
GN_EPS = 64e-5
CONV_W = 3

LANES = 128
SUBLANES = 8
VMEM_LIMIT = 56 * 1024 * 1024
NEG = -1e30
INT_MIN = -(2 ** 31)

LOG2E = 1.4426950408889634
TQ = 128
KB = 512
ATT_GROUPS = 2
SCAN_L = 64
NT = (((1,), (1,)), ((), ()))


def _dot(a, b):
    return jnp.dot(a, b, preferred_element_type=F32)


def _dot_nt(a, b):
    return lax.dot_general(a, b, NT, preferred_element_type=F32)


def _split(x):
    hi = x.astype(BF16)
    lo = (x - hi.astype(F32)).astype(BF16)
    return hi, lo


def _sigmoid(x):
    return 1.0 / (1.0 + jnp.exp(-x))


def _rms(x, gain):
    return x * lax.rsqrt(jnp.mean(x * x, axis=-1, keepdims=True) + EPS) * gain


def _rms_mod(x, gain, sc, sh):
    return _rms(x, gain) * (1.0 + sc) + sh


def _params(*sem):
    return pltpu.CompilerParams(dimension_semantics=sem, vmem_limit_bytes=VMEM_LIMIT)


def _full(shape, single=False):
    n = len(shape)
    mode = dict(pipeline_mode=pl.Buffered(1)) if single else {}
    return pl.BlockSpec(shape, lambda *_: (0,) * n, **mode)


def _ada_body(c_ref, w_ref, b_ref, o_ref):
    c = c_ref[...]
    o_ref[0] = _dot(c * _sigmoid(c), w_ref[0]) + b_ref[0]


def _ada(c, ada_w, ada_b):
    depth, d, n = ada_w.shape
    b = c.shape[0]
    tn = n // 4
    return pl.pallas_call(
        _ada_body,
        grid=(depth, n // tn),
        in_specs=[pl.BlockSpec((b, d), lambda i, j: (0, 0)),
                  pl.BlockSpec((1, d, tn), lambda i, j: (i, 0, j)),
                  pl.BlockSpec((1, 1, tn), lambda i, j: (i, 0, j))],
        out_specs=pl.BlockSpec((1, b, tn), lambda i, j: (i, 0, j)),
        out_shape=jax.ShapeDtypeStruct((depth, b, n), F32),
        name="ada_mod",
        compiler_params=_params("arbitrary", "arbitrary"),
    )(c, ada_w, ada_b.reshape(depth, 1, n))


def _ffn_body(final_norm, nf, h_ref, sh_ref, sc_ref, g_ref, gain_ref, wup_ref, cw_ref, cb_ref,
              wdn_ref, gfin_ref, o_ref, carry_ref, hn_ref, act_ref):
    tm = h_ref.shape[1]
    tf = wup_ref.shape[3]

    @pl.when(pl.program_id(1) == 0)
    def _():
        carry_ref[...] = jnp.zeros_like(carry_ref)

    hn_ref[...] = _rms_mod(h_ref[0], gain_ref[...], sc_ref[0], sh_ref[0]).astype(BF16)
    rid = lax.broadcasted_iota(I32, (tm, 1), 0)

    def conv(part, j):
        u = _dot(hn_ref[...], wup_ref[part, j])
        prev = carry_ref[part, j]
        carry_ref[part, j] = u[tm - SUBLANES:, :]
        p1 = prev[SUBLANES - 1:SUBLANES, :]
        p2 = prev[SUBLANES - 2:SUBLANES - 1, :]
        s1 = jnp.where(rid == 0, p1, pltpu.roll(u, 1, 0))
        s2 = jnp.where(rid == 0, p2, jnp.where(rid == 1, p1, pltpu.roll(u, 2, 0)))
        cw = cw_ref[part, j]
        return cw[2:3] * u + cw[1:2] * s1 + cw[0:1] * s2 + cb_ref[part, j]

    for j in range(nf):
        ug = conv(0, j)
        uv = conv(1, j)
        act_ref[:, j * tf:(j + 1) * tf] = (ug * _sigmoid(ug) * uv).astype(BF16)
    out = h_ref[0] + g_ref[0] * _dot(act_ref[...], wdn_ref[...])
    if final_norm:
        out = _rms(out, gfin_ref[...])
    o_ref[0] = out


def _ffn(h, sh, sc, g, gain, w_up, conv_w, conv_b, w_down, gfin, final_norm, tm=512, tf=256):
    b, s, d = h.shape
    f = w_down.shape[0]
    nf = f // tf
    tm = min(tm, s)
    wup = w_up.astype(BF16).reshape(d, 2, nf, tf).transpose(1, 2, 0, 3)
    cw = conv_w.reshape(CONV_W, 2, nf, tf).transpose(1, 2, 0, 3)
    cb = conv_b.reshape(2, nf, 1, tf)
    wdn = w_down.astype(BF16)
    row = pl.BlockSpec((1, 1, d), lambda bi, i: (bi, 0, 0))
    tile = pl.BlockSpec((1, tm, d), lambda bi, i: (bi, i, 0))
    return pl.pallas_call(
        functools.partial(_ffn_body, final_norm, nf),
        grid=(b, s // tm),
        in_specs=[tile, row, row, row, _full((1, d)), _full(wup.shape, True), _full(cw.shape),
                  _full(cb.shape), _full(wdn.shape, True), _full((1, d))],
        out_specs=tile,
        out_shape=jax.ShapeDtypeStruct((b, s, d), F32),
        scratch_shapes=[pltpu.VMEM((2, nf, SUBLANES, tf), F32),
                        pltpu.VMEM((tm, d), BF16),
                        pltpu.VMEM((tm, f), BF16)],
        name="conv_ffn",
        compiler_params=_params("arbitrary", "arbitrary"),
    )(h, sh, sc, g, gain.reshape(1, d), wup, cw, cb, wdn, gfin.reshape(1, d))


def _fold_body(wuq_ref, wuk_ref, wuv_ref, wo_ref, wq_ref, wvo_ref):
    hi = lax.Precision.HIGHEST
    wq_ref[0] = lax.dot_general(wuq_ref[0], wuk_ref[0], NT, precision=hi,
                                preferred_element_type=F32) * (A_HEAD_DIM ** -0.5 * LOG2E)
    wvo_ref[0] = jnp.dot(wuv_ref[0], wo_ref[0], precision=hi, preferred_element_type=F32)


def _fold(w_uq, w_uk, w_uv, w_o):
    rq = w_uq.shape[0]
    h, c, dh = w_uk.shape
    d = w_o.shape[1]
    wuq = w_uq.reshape(rq, h, dh).transpose(1, 0, 2)
    wo = w_o.reshape(h, dh, d)
    per_head = lambda *shape: pl.BlockSpec((1,) + shape, lambda i: (i, 0, 0))
    wq, wvo = pl.pallas_call(
        _fold_body,
        grid=(h,),
        in_specs=[per_head(rq, dh), per_head(c, dh), per_head(c, dh), per_head(dh, d)],
        out_specs=[per_head(rq, c), per_head(c, d)],
        out_shape=[jax.ShapeDtypeStruct((h, rq, c), F32), jax.ShapeDtypeStruct((h, c, d), F32)],
        name="dsa_fold",
        compiler_params=_params("arbitrary"),
    )(wuq, w_uk, w_uv, wo)
    return (wq.transpose(1, 0, 2).reshape(rq, h * c).astype(BF16),
            wvo.reshape(h * c, d).astype(BF16))


def _dsa_proj_body(x_ref, sh_ref, sc_ref, gain_ref, win_ref, qn_ref, kvn_ref, kg_ref, kb_ref,
                   wq_ref, wqi_ref, qabs_ref, qidx_ref, aux_ref, kpad_ref, caug_ref):
    tm = x_ref.shape[1]
    hn = _rms_mod(x_ref[0], gain_ref[...], sc_ref[0], sh_ref[0]).astype(BF16)
    proj = _dot(hn, win_ref[...])
    ql = _rms(proj[:, :A_Q_RANK], qn_ref[...]).astype(BF16)
    ck = _rms(proj[:, A_Q_RANK:A_Q_RANK + A_KV_RANK], kvn_ref[...])
    slab = proj[:, A_Q_RANK + A_KV_RANK:]
    lane = lax.broadcasted_iota(I32, (1, LANES), 1)
    is_k = lane < IDX_DIM
    mu = jnp.sum(jnp.where(is_k, slab, 0.0), axis=-1, keepdims=True) * (1.0 / IDX_DIM)
    dk = jnp.where(is_k, slab - mu, 0.0)
    var = jnp.sum(dk * dk, axis=-1, keepdims=True) * (1.0 / IDX_DIM)
    kpad_ref[0] = (dk * lax.rsqrt(var + EPS) * kg_ref[...] + kb_ref[...]).astype(BF16)
    aux_ref[0] = slab * (IDX_HEADS ** -0.5 * IDX_DIM ** -0.5)
    qabs_ref[0] = _dot(ql, wq_ref[...]).astype(BF16)
    qidx_ref[0] = _dot(ql, wqi_ref[...]).astype(BF16)
    pos = pl.program_id(1) * tm + lax.broadcasted_iota(I32, (tm, 1), 0)
    p_hi = (pos >> 6).astype(F32)
    p_lo = (pos & 63).astype(F32)
    extra = jnp.where(lane < 2, p_hi, jnp.where(lane < 4, p_lo, 0.0))
    caug_ref[0, :, :A_KV_RANK] = ck.astype(BF16)
    caug_ref[0, :, A_KV_RANK:] = extra.astype(BF16)


def _dsa_proj(x, sh, sc, gain, w_in, q_norm, kv_norm, kidx_g, kidx_b, wq, w_qi, tm=512):
    b, s, d = x.shape
    tm = min(tm, s)
    n_in = w_in.shape[1]
    win = jnp.pad(w_in, ((0, 0), (0, 4 * LANES - n_in))).astype(BF16)
    wqi = jnp.pad(w_qi.reshape(A_Q_RANK, IDX_HEADS, IDX_DIM),
                  ((0, 0), (0, 0), (0, LANES - IDX_DIM))).reshape(A_Q_RANK, IDX_HEADS * LANES).astype(BF16)
    kg = jnp.pad(kidx_g, (0, LANES - IDX_DIM)).reshape(1, LANES)
    kb = jnp.pad(kidx_b, (0, LANES - IDX_DIM)).reshape(1, LANES)
    row = pl.BlockSpec((1, 1, d), lambda bi, i: (bi, 0, 0))
    tile = lambda n: pl.BlockSpec((1, tm, n), lambda bi, i: (bi, i, 0))
    nq, ni = A_HEADS * A_KV_RANK, IDX_HEADS * LANES
    return pl.pallas_call(
        _dsa_proj_body,
        grid=(b, s // tm),
        in_specs=[tile(d), row, row, _full((1, d)), _full(win.shape), _full((1, A_Q_RANK)),
                  _full((1, A_KV_RANK)), _full((1, LANES)), _full((1, LANES)),
                  _full(wq.shape), _full(wqi.shape)],
        out_specs=[tile(nq), tile(ni), tile(LANES), tile(LANES), tile(2 * A_KV_RANK)],
        out_shape=[jax.ShapeDtypeStruct((b, s, nq), BF16), jax.ShapeDtypeStruct((b, s, ni), BF16),
                   jax.ShapeDtypeStruct((b, s, LANES), F32), jax.ShapeDtypeStruct((b, s, LANES), BF16),
                   jax.ShapeDtypeStruct((b, s, 2 * A_KV_RANK), BF16)],
        name="dsa_proj",
        compiler_params=_params("arbitrary", "arbitrary"),
    )(x, sh, sc, gain.reshape(1, d), win, q_norm.reshape(1, -1), kv_norm.reshape(1, -1), kg, kb, wq, wqi)


def _bf16_digits(v):
    hi = float(np.asarray(v, np.float32).astype(BF16).astype(np.float32))
    lo = float(np.asarray(v - hi, np.float32).astype(BF16).astype(np.float32))
    return hi, lo


def _dsa_attn_body(x_ref, gm_ref, qabs_ref, qidx_ref, aux_ref, kpad_ref, caug_ref, wvo_ref, o_ref,
                   keys_ref, keyst_ref, qaug_ref, qis_ref, wb_ref, tab_ref, t_ref, m_ref, al_ref, l_ref,
                   acc_ref, s_ref, p_ref):
    tq = x_ref.shape[1]
    seq = kpad_ref.shape[1]
    c = A_KV_RANK
    nj = KB // LANES
    q0 = pl.program_id(1) * tq
    kend = q0 + tq
    n_kb = (kend + KB - 1) // KB
    tail = (n_kb - 1) * KB
    lane = lax.broadcasted_iota(I32, (1, LANES), 1)
    trow = lax.broadcasted_iota(I32, (tq, 1), 0)
    slopes = [LOG2E * 2.0 ** (-8.0 * (h + 1) / A_HEADS) for h in range(A_HEADS)]

    kcol = tail + lax.broadcasted_iota(I32, (tq, KB), 1)
    ahead = jnp.maximum(kcol - (q0 + trow), 0).astype(F32)
    admissible = (kcol >> 6) <= ((q0 + trow) >> 6)
    for h in range(A_HEADS):
        rows = slice(h * tq, (h + 1) * tq)
        hi, lo = _bf16_digits(slopes[h])
        digits = jnp.where(lane == 0, 64.0 * hi, jnp.where(lane == 1, 64.0 * lo,
                           jnp.where(lane == 2, hi, jnp.where(lane == 3, lo, 0.0))))
        qaug_ref[rows, :c] = qabs_ref[0, :, h * c:(h + 1) * c]
        qaug_ref[rows, c:] = jnp.broadcast_to(digits, (tq, LANES)).astype(BF16)
        tab_ref[rows, :] = jnp.where(admissible, (-2.0 * slopes[h]) * ahead, NEG)
    aux = aux_ref[0]
    for h in range(IDX_HEADS):
        qis_ref[h * tq:(h + 1) * tq, :] = qidx_ref[0, :, h * LANES:(h + 1) * LANES]
        wb_ref[h] = jnp.broadcast_to(aux[:, IDX_DIM + h:IDX_DIM + h + 1], (tq, LANES))

    qchunk = (q0 + trow) >> 6

    def to_key(sc):
        bits = pltpu.bitcast(sc, I32)
        return bits ^ ((bits >> 31) & 0x7FFFFFFF)

    def score_block(kb, carry):
        off = pl.multiple_of(kb * KB, KB)
        z = _dot_nt(qis_ref[...], kpad_ref[0, pl.ds(off, KB), :])
        for j in range(nj):
            sc = jnp.zeros((tq, LANES), F32)
            for h in range(IDX_HEADS):
                sc = sc + wb_ref[h] * jnp.maximum(z[h * tq:(h + 1) * tq, j * LANES:(j + 1) * LANES], 0.0)
            kchunk = (off + j * LANES + lane) >> 6
            sc = jnp.where(kchunk <= qchunk, sc, -jnp.inf)
            keys_ref[kb * nj + j] = to_key(sc)
            keyst_ref[pl.ds(pl.multiple_of(off + j * LANES, LANES), LANES), :] = to_key(sc.T)
        return carry

    lax.fori_loop(0, n_kb, score_block, 0)

    slab = 8 * SUBLANES
    krow = lax.broadcasted_iota(I32, (slab, LANES), 0)

    def count(pred):
        def body(kb, acc):
            for u in range(KB // slab):
                r0 = pl.multiple_of(kb * KB + u * slab, slab)
                acc = acc + jnp.where(pred(keyst_ref[pl.ds(r0, slab), :], r0 + krow), 1.0, 0.0)
            return acc
        part = lax.fori_loop(0, n_kb, body, jnp.zeros((slab, LANES), F32))
        return jnp.sum(part, axis=0, keepdims=True)

    def per_row(v):
        return pltpu.bitcast(pltpu.bitcast(jnp.broadcast_to(v, (LANES, LANES)), F32).T, I32)

    t_ref[...] = jnp.full(t_ref.shape, INT_MIN, I32)

    @pl.when(kend > TOPK_MAX)
    def _():
        kf = float(TOPK_MAX)
        n_pos = count(lambda k, s: k >= 0)
        t0 = jnp.where(n_pos >= kf, 0, INT_MIN).astype(I32)

        def bit_step(i, t):
            cand = t | jnp.left_shift(jnp.int32(1), 30 - i)
            return jnp.where(count(lambda k, s: k >= cand) >= kf, cand, t)

        t = lax.fori_loop(0, 31, bit_step, t0)
        t_ref[...] = per_row(t)
        n_ge = count(lambda k, s: k >= t)

        @pl.when(jnp.max(n_ge) > kf)
        def _():
            need = kf - count(lambda k, s: k > t)
            n_bits = max(1, int(seq - 1).bit_length())

            def idx_step(i, x):
                cand = x | jnp.left_shift(jnp.int32(1), n_bits - 1 - i)
                below = count(lambda k, s: (k == t) & (s < cand))
                return jnp.where(below < need, cand, x)

            last = per_row(lax.fori_loop(0, n_bits, idx_step, jnp.zeros((1, LANES), I32)))
            thr_ = t_ref[...]
            demoted = jnp.where(thr_ == INT_MIN, thr_, thr_ - 1)

            def demote(j, carry):
                k = keys_ref[j]
                keys_ref[j] = jnp.where((k == thr_) & (j * LANES + lane > last), demoted, k)
                return carry

            lax.fori_loop(0, n_kb * nj, demote, 0)

    m_ref[...] = jnp.full(m_ref.shape, NEG, F32)
    l_ref[...] = jnp.zeros_like(l_ref)
    acc_ref[...] = jnp.zeros_like(acc_ref)
    thr = t_ref[...]
    hg = A_HEADS // ATT_GROUPS

    def attend(kb, is_tail):
        cb = caug_ref[0, pl.ds(pl.multiple_of(kb * KB, KB), KB), :]
        group = lambda g: slice(g * hg * tq, (g + 1) * hg * tq)
        for g in range(ATT_GROUPS):
            s_ref[group(g), :] = _dot_nt(qaug_ref[group(g), :], cb)
        bias = [jnp.where(keys_ref[kb * nj + j] >= thr, 0.0, NEG) for j in range(nj)]
        for g in range(ATT_GROUPS):
            for h in range(g * hg, (g + 1) * hg):
                rows = slice(h * tq, (h + 1) * tq)
                tiles = []
                for j in range(nj):
                    cols = slice(j * LANES, (j + 1) * LANES)
                    t_ = s_ref[rows, cols] + bias[j]
                    tiles.append(t_ + tab_ref[rows, cols] if is_tail else t_)
                top = functools.reduce(jnp.maximum, tiles)
                m_old = m_ref[rows, :]
                m_new = jnp.maximum(m_old, jnp.broadcast_to(jnp.max(top, axis=-1, keepdims=True), top.shape))
                ps = [jnp.exp2(t_ - m_new) for t_ in tiles]
                p_ref[rows, :] = jnp.concatenate(ps, axis=1).astype(BF16)
                alpha = jnp.exp2(m_old - m_new)
                l_ref[rows, :] = l_ref[rows, :] * alpha + functools.reduce(jnp.add, ps)
                al_ref[rows, :] = alpha
                m_ref[rows, :] = m_new
            acc_ref[group(g), :] = (acc_ref[group(g), :] * al_ref[group(g), :]
                                    + _dot(p_ref[group(g), :], cb[:, :c]))

    def past_block(kb, carry):
        attend(kb, False)
        return carry

    lax.fori_loop(0, n_kb - 1, past_block, 0)
    attend(n_kb - 1, True)

    y = jnp.zeros((tq, x_ref.shape[2]), F32)
    for pair in range(A_HEADS // 2):
        parts = []
        for h in (2 * pair, 2 * pair + 1):
            rows = slice(h * tq, (h + 1) * tq)
            parts.append((acc_ref[rows, :] / jnp.sum(l_ref[rows, :], axis=-1, keepdims=True)).astype(BF16))
        y = y + _dot(jnp.concatenate(parts, axis=1), wvo_ref[2 * pair * c:(2 * pair + 2) * c, :])
    o_ref[0] = x_ref[0] + gm_ref[0] * y


def _dsa_attn(x, gm, qabs, qidx, aux, kpad, caug, wvo):
    b, s, d = x.shape
    tq = TQ
    assert tq == LANES and s % KB == 0
    hq = A_HEADS * tq
    row = pl.BlockSpec((1, 1, d), lambda bi, i: (bi, 0, 0))
    tile = lambda n: pl.BlockSpec((1, tq, n), lambda bi, i: (bi, i, 0))
    per_batch = lambda n: pl.BlockSpec((1, s, n), lambda bi, i: (bi, 0, 0))
    return pl.pallas_call(
        _dsa_attn_body,
        grid=(b, s // tq),
        in_specs=[tile(d), row, tile(qabs.shape[2]), tile(qidx.shape[2]), tile(LANES),
                  per_batch(LANES), per_batch(2 * A_KV_RANK), _full(wvo.shape)],
        out_specs=tile(d),
        out_shape=jax.ShapeDtypeStruct((b, s, d), F32),
        scratch_shapes=[pltpu.VMEM((s // LANES, tq, LANES), I32),
                        pltpu.VMEM((s, tq), I32),
                        pltpu.VMEM((hq, 2 * A_KV_RANK), BF16),
                        pltpu.VMEM((IDX_HEADS * tq, LANES), BF16),
                        pltpu.VMEM((IDX_HEADS, tq, LANES), F32),
                        pltpu.VMEM((hq, KB), F32),
                        pltpu.VMEM((tq, LANES), I32),
                        pltpu.VMEM((hq, LANES), F32),
                        pltpu.VMEM((hq, LANES), F32),
                        pltpu.VMEM((hq, LANES), F32),
                        pltpu.VMEM((hq, A_KV_RANK), F32),
                        pltpu.VMEM((hq, KB), F32),
                        pltpu.VMEM((hq, KB), BF16)],
        name="dsa_attn",
        compiler_params=_params("arbitrary", "arbitrary"),
    )(x, gm, qabs, qidx, aux, kpad, caug, wvo)


def _rwkv_proj_body(h_ref, sh_ref, sc_ref, gain_ref, mu_ref, wr_ref, wk_ref, wv_ref, w1_ref, w2_ref,
                    a1_ref, a2_ref, g1_ref, g2_ref, w0_ref, a0_ref,
                    r_ref, k_ref, v_ref, lw_ref, a_ref, g_ref, carry_ref):
    tm = h_ref.shape[1]

    @pl.when(pl.program_id(1) == 0)
    def _():
        carry_ref[...] = jnp.zeros_like(carry_ref)

    hn = _rms_mod(h_ref[0], gain_ref[...], sc_ref[0], sh_ref[0])
    prev = carry_ref[SUBLANES - 1:SUBLANES, :]
    carry_ref[...] = hn[tm - SUBLANES:, :]
    rid = lax.broadcasted_iota(I32, (tm, 1), 0)
    dx = jnp.where(rid == 0, prev, pltpu.roll(hn, 1, 0)) - hn
    mix = lambda j: (hn + dx * mu_ref[j:j + 1, :]).astype(BF16)

    def put(ref, val):
        for p in range(val.shape[1] // LANES):
            ref[0, p] = val[:, p * LANES:(p + 1) * LANES]

    put(r_ref, _dot(mix(0), wr_ref[...]))
    put(k_ref, _dot(mix(2), wk_ref[...]))
    put(v_ref, _dot(mix(3), wv_ref[...]))
    ww = w0_ref[...] + _dot(jnp.tanh(_dot(mix(1), w1_ref[...])).astype(BF16), w2_ref[...])
    softplus = jnp.maximum(-ww, 0.0) + jnp.log(1.0 + jnp.exp(-jnp.abs(ww)))
    put(lw_ref, -jnp.exp(-softplus - 0.5))
    put(a_ref, _sigmoid(a0_ref[...] + _dot(_dot(mix(4), a1_ref[...]).astype(BF16), a2_ref[...])))
    g_ref[0] = _dot(_sigmoid(_dot(mix(5), g1_ref[...])).astype(BF16), g2_ref[...])


def _pad_rank(w_down, w_up):
    r = w_down.shape[1]
    p = -r % LANES
    return (jnp.pad(w_down, ((0, 0), (0, p))).astype(BF16), jnp.pad(w_up, ((0, p), (0, 0))).astype(BF16))


def _rwkv_proj(h, sh, sc, gain, mu, w_r, w_k, w_v, w0, w1, w2, a0, a1, a2, g1, g2, tm=512):
    b, s, d = h.shape
    tm = min(tm, s)
    w1p, w2p = _pad_rank(w1, w2)
    a1p, a2p = _pad_rank(a1, a2)
    g1p, g2p = _pad_rank(g1, g2)
    row = pl.BlockSpec((1, 1, d), lambda bi, i: (bi, 0, 0))
    tile = pl.BlockSpec((1, tm, d), lambda bi, i: (bi, i, 0))
    big = [w.astype(BF16) for w in (w_r, w_k, w_v)]
    small = [w1p, w2p, a1p, a2p, g1p, g2p]
    np_ = d // LANES
    ptile = pl.BlockSpec((1, np_, tm, LANES), lambda bi, i: (bi, 0, i, 0))
    pout = jax.ShapeDtypeStruct((b, np_, s, LANES), F32)
    return pl.pallas_call(
        _rwkv_proj_body,
        grid=(b, s // tm),
        in_specs=[tile, row, row, _full((1, d)), _full(mu.shape)] + [_full(w.shape) for w in big + small]
                 + [_full((1, d)), _full((1, d))],
        out_specs=[ptile] * 5 + [tile],
        out_shape=[pout] * 5 + [jax.ShapeDtypeStruct((b, s, d), F32)],
        scratch_shapes=[pltpu.VMEM((SUBLANES, d), F32)],
        name="rwkv_proj",
        compiler_params=_params("arbitrary", "arbitrary"),
    )(h, sh, sc, gain.reshape(1, d), mu, *big, *small, w0.reshape(1, d), a0.reshape(1, d))


def _rwkv_scan_body(n_chunks, r_ref, k_ref, v_ref, lw_ref, a_ref, kk_ref, ka_ref, rk_ref, gg_ref, gb_ref,
                    y_ref, st_ref, w_s, rt_s, u0_s, y0_s, bon_s, arb_s, bkt_s, plt_s):
    L = SCAN_L
    P2 = 2 * L
    n_pairs = r_ref.shape[1]

    @pl.when(pl.program_id(1) == 0)
    def _():
        st_ref[...] = jnp.zeros_like(st_ref)

    lane = lax.broadcasted_iota(I32, (1, LANES), 1)
    head0 = lane < B_HEAD
    m0 = jnp.where(head0, 1.0, 0.0)
    m1 = 1.0 - m0
    ri = lax.broadcasted_iota(I32, (P2, P2), 0)
    ci = lax.broadcasted_iota(I32, (P2, P2), 1)
    same = (ri >> 6) == (ci >> 6)
    strict = same & ((ci & 63) < (ri & 63))
    incl = same & ((ci & 63) <= (ri & 63))
    eye = jnp.where(ri == ci, 1.0, 0.0)
    tril = jnp.where(lax.broadcasted_iota(I32, (L, L), 1) <= lax.broadcasted_iota(I32, (L, L), 0),
                     1.0, 0.0).astype(BF16)
    def seg_sum(x):
        s0 = jnp.sum(x * m0, axis=-1, keepdims=True)
        s1 = jnp.sum(x * m1, axis=-1, keepdims=True)
        return jnp.where(head0, s0, s1)

    stack = lambda x: jnp.concatenate([x * m0, x * m1], axis=0)
    fold = lambda x: x[:L] + x[L:]

    bdot = lambda a, b: _dot(a.astype(BF16), b.astype(BF16))

    each = lambda f, *xs: [f(*a) for a in zip(*xs)]

    def prepare_pair(p, carry):
        k_k, k_a, r_k = kk_ref[p], ka_ref[p], rk_ref[p]
        cs = range(n_chunks)
        r, kraw, v, lw, asig = ([ref[0, p, c * L:(c + 1) * L, :] for c in cs]
                                for ref in (r_ref, k_ref, v_ref, lw_ref, a_ref))
        kk = each(lambda x: x * k_k, kraw)
        kkn = each(lambda x: x / jnp.maximum(jnp.sqrt(seg_sum(x * x)), 1e-12), kk)
        kmod = each(lambda x, a: x * (1.0 + (a - 1.0) * k_a), kraw, asig)
        bvec = each(jnp.multiply, kkn, asig)
        gc = each(lambda x: _dot(tril, jnp.concatenate(_split(x), axis=1)), lw)
        g = each(lambda x: x[:, :LANES] + x[:, LANES:], gc)
        g_last = each(lambda x: x[L - 1:L, :], g)
        at = each(lambda kn, x, w: -kn * jnp.exp(x - w), kkn, g, lw)
        rt = each(lambda x, y: x * jnp.exp(y), r, g)
        inv = each(lambda x: jnp.exp(-x), g)
        to_end = each(lambda x, y: jnp.exp(x - y), g_last, g)
        bk_end = each(lambda b_, k_, e: jnp.concatenate([b_ * e, k_ * e], axis=0), bvec, kmod, to_end)
        lhs = each(lambda a, b_: jnp.concatenate([stack(a), stack(b_)], axis=0).astype(BF16), at, rt)
        rhs = each(lambda b_, k_, i: jnp.concatenate([b_ * i, b_ * i, k_ * i, k_ * i], axis=0).astype(BF16),
                   bvec, kmod, inv)
        mm = each(_dot_nt, lhs, rhs)
        n_ab = each(lambda x: jnp.where(strict, x[:P2, :P2], 0.0), mm)
        a_kk = each(lambda x: jnp.concatenate([jnp.where(strict, x[:P2, P2:], 0.0),
                                               jnp.where(incl, x[P2:, P2:], 0.0)], axis=0), mm)
        a_rb = each(lambda x: jnp.where(incl, x[P2:, :P2], 0.0), mm)

        tmat = each(lambda x: eye + x, n_ab)
        pw = n_ab
        for _ in range(int(L).bit_length() - 2):
            pw = each(bdot, pw, pw)
            tmat = each(lambda t, q: t + bdot(t, q), tmat, pw)

        akv = each(lambda a, x: bdot(a, stack(x)), a_kk, v)
        tw = each(lambda t, a, x: bdot(t, jnp.concatenate([stack(a), stack(fold(x[:P2]))], axis=1)),
                  tmat, at, akv)
        for c in cs:
            w_s[p, c] = fold(tw[c][:, :LANES]).astype(BF16)
            rt_s[p, c] = rt[c].astype(BF16)
            u0_s[p, c] = fold(tw[c][:, LANES:])
            y0_s[p, c] = fold(akv[c][P2:])
            bon_s[p, c] = seg_sum(r[c] * kmod[c] * r_k) * v[c]
            arb_s[p, c] = a_rb[c].astype(BF16)
            bkt_s[p, c] = bk_end[c].T.astype(BF16)
            plt_s[p, c] = jnp.broadcast_to(jnp.exp(g_last[c]), (P2, LANES)).T
        return carry

    lax.fori_loop(0, n_pairs, prepare_pair, 0)

    def advance(c, carry):
        rows = pl.ds(pl.multiple_of(c * L, L), L)
        ps = range(n_pairs)
        st = [st_ref[p] for p in ps]
        ws = [_dot(jnp.concatenate([w_s[p, c], rt_s[p, c]], axis=0), st[p].astype(BF16)) for p in ps]
        u = [ws[p][:L] + u0_s[p, c] for p in ps]
        uv = [jnp.concatenate([u[p], v_ref[0, p, rows, :]], axis=0).astype(BF16) for p in ps]
        upd = [_dot(bkt_s[p, c], uv[p]) for p in ps]
        for p in ps:
            st_ref[p] = st[p] * plt_s[p, c] + jnp.where(same, upd[p], 0.0)
        au = [_dot(arb_s[p, c], stack(u[p]).astype(BF16)) for p in ps]
        for p in ps:
            y = ws[p][L:] + fold(au[p]) + y0_s[p, c]
            mu = seg_sum(y) * (1.0 / B_HEAD)
            dy = y - mu
            var = seg_sum(dy * dy) * (1.0 / B_HEAD)
            y_ref[0, p, rows, :] = dy * lax.rsqrt(var + GN_EPS) * gg_ref[p] + gb_ref[p] + bon_s[p, c]
        return carry

    lax.fori_loop(0, n_chunks, advance, 0)


def _rwkv_scan(r, k, v, lw, a, k_k, k_a, r_k, gn_g, gn_b, lb=512):
    b, n_pairs, s, _ = r.shape
    lb = min(lb, s)
    nc = lb // SCAN_L
    tile = pl.BlockSpec((1, n_pairs, lb, LANES), lambda bi, i: (bi, 0, i, 0))
    vec = lambda x: x.reshape(n_pairs, 1, LANES)
    per_chunk = lambda rows, dt: pltpu.VMEM((n_pairs, nc, rows, LANES), dt)
    return pl.pallas_call(
        functools.partial(_rwkv_scan_body, nc),
        grid=(b, s // lb),
        in_specs=[tile] * 5 + [_full((n_pairs, 1, LANES))] * 5,
        out_specs=tile,
        out_shape=jax.ShapeDtypeStruct(r.shape, F32),
        scratch_shapes=[pltpu.VMEM((n_pairs, LANES, LANES), F32),
                        per_chunk(SCAN_L, BF16), per_chunk(SCAN_L, BF16),
                        per_chunk(SCAN_L, F32), per_chunk(SCAN_L, F32), per_chunk(SCAN_L, F32),
                        per_chunk(2 * SCAN_L, BF16), per_chunk(2 * SCAN_L, BF16),
                        per_chunk(2 * SCAN_L, F32)],
        name="rwkv_scan",
        compiler_params=_params("arbitrary", "arbitrary"),
    )(r, k, v, lw, a, vec(k_k), vec(k_a), vec(r_k), vec(gn_g), vec(gn_b))


def _rwkv_out_body(h_ref, gm_ref, y_ref, g_ref, wo_ref, o_ref):
    y = jnp.concatenate([y_ref[0, p] for p in range(y_ref.shape[1])], axis=1)
    o_ref[0] = h_ref[0] + gm_ref[0] * _dot((y * g_ref[0]).astype(BF16), wo_ref[...])


def _rwkv_out(h, gm, y, g, w_o, tm=512):
    b, s, d = h.shape
    tm = min(tm, s)
    row = pl.BlockSpec((1, 1, d), lambda bi, i: (bi, 0, 0))
    tile = pl.BlockSpec((1, tm, d), lambda bi, i: (bi, i, 0))
    ptile = pl.BlockSpec((1, d // LANES, tm, LANES), lambda bi, i: (bi, 0, i, 0))
    return pl.pallas_call(
        _rwkv_out_body,
        grid=(b, s // tm),
        in_specs=[tile, row, ptile, tile, _full((d, d))],
        out_specs=tile,
        out_shape=jax.ShapeDtypeStruct((b, s, d), F32),
        name="rwkv_out",
        compiler_params=_params("arbitrary", "arbitrary"),
    )(h, gm, y, g, w_o.astype(BF16))


def kernel(x, c, ada_w, ada_b, norm_mix, norm_ffn, norm_final, a_w_in, a_q_norm, a_kv_norm, a_kidx_g, a_kidx_b, a_w_uq, a_w_qi, a_w_uk, a_w_uv, a_w_o, b_mu, b_w_r, b_w_k, b_w_v, b_w_o, b_w0, b_w1, b_w2, b_a0, b_a1, b_a2, b_g1, b_g2, b_k_k, b_k_a, b_r_k, b_gn_g, b_gn_b, f_w_up, f_conv_w, f_conv_b, f_w_down):
    b, s, d = x.shape
    mod = _ada(c, ada_w, ada_b).reshape(ada_w.shape[0], b, 6, 1, d)
    sh_m, sc_m, g_m, sh_f, sc_f, g_f = (mod[:, :, j] for j in range(6))

    wq, wvo = _fold(a_w_uq[0], a_w_uk[0], a_w_uv[0], a_w_o[0])
    qabs, qidx, aux, kpad, caug = _dsa_proj(x, sh_m[0], sc_m[0], norm_mix[0], a_w_in[0], a_q_norm[0],
                                            a_kv_norm[0], a_kidx_g[0], a_kidx_b[0], wq, a_w_qi[0])
    h = _dsa_attn(x, g_m[0], qabs, qidx, aux, kpad, caug, wvo)
    h = _ffn(h, sh_f[0], sc_f[0], g_f[0], norm_ffn[0], f_w_up[0], f_conv_w[0], f_conv_b[0], f_w_down[0],
             norm_final, False)

    r, k, v, lw, a, g = _rwkv_proj(h, sh_m[1], sc_m[1], norm_mix[1], b_mu[0], b_w_r[0], b_w_k[0], b_w_v[0],
                                   b_w0[0], b_w1[0], b_w2[0], b_a0[0], b_a1[0], b_a2[0], b_g1[0], b_g2[0])
    y = _rwkv_scan(r, k, v, lw, a, b_k_k[0], b_k_a[0], b_r_k[0].reshape(-1), b_gn_g[0], b_gn_b[0])
    h = _rwkv_out(h, g_m[1], y, g, b_w_o[0])
    return _ffn(h, sh_f[1], sc_f[1], g_f[1], norm_ffn[1], f_w_up[1], f_conv_w[1], f_conv_b[1], f_w_down[1],
                norm_final, True)
```

```python
import functools

import numpy as np
import jax
import jax.numpy as jnp
from jax import lax
from jax.experimental import pallas as pl
from jax.experimental.pallas import tpu as pltpu

F32 = jnp.float32
BF16 = jnp.bfloat16
I32 = jnp.int32

EPS = 1e-6
CHUNK = 64
A_HEADS = 16
A_HEAD_DIM = 64
A_Q_RANK = 256
A_KV_RANK = 128
IDX_HEADS = 8
IDX_DIM = 64
TOPK_MAX = 256
B_HEAD = 64
GN_EPS = 64e-5
CONV_W = 3

LANES = 128
SUBLANES = 8
VMEM_LIMIT = 56 * 1024 * 1024
NEG = -1e30
INT_MIN = -(2 ** 31)

LOG2E = 1.4426950408889634
TQ = 128
KB = 512
ATT_GROUPS = 4
SCAN_L = 64
NT = (((1,), (1,)), ((), ()))


def _dot(a, b):
    return jnp.dot(a, b, preferred_element_type=F32)


def _dot_nt(a, b):
    return lax.dot_general(a, b, NT, preferred_element_type=F32)


def _split(x):
    hi = x.astype(BF16)
    lo = (x - hi.astype(F32)).astype(BF16)
    return hi, lo


def _sigmoid(x):
    return 1.0 / (1.0 + jnp.exp(-x))


def _rms(x, gain):
    return x * lax.rsqrt(jnp.mean(x * x, axis=-1, keepdims=True) + EPS) * gain


def _rms_mod(x, gain, sc, sh):
    return _rms(x, gain) * (1.0 + sc) + sh


def _params(*sem):
    return pltpu.CompilerParams(dimension_semantics=sem, vmem_limit_bytes=VMEM_LIMIT)


def _full(shape, single=False):
    n = len(shape)
    mode = dict(pipeline_mode=pl.Buffered(1)) if single else {}
    return pl.BlockSpec(shape, lambda *_: (0,) * n, **mode)


def _ada_body(c_ref, w_ref, b_ref, o_ref):
    c = c_ref[...]
    o_ref[0] = _dot(c * _sigmoid(c), w_ref[0]) + b_ref[0]


def _ada(c, ada_w, ada_b):
    depth, d, n = ada_w.shape
    b = c.shape[0]
    tn = n // 4
    return pl.pallas_call(
        _ada_body,
        grid=(depth, n // tn),
        in_specs=[pl.BlockSpec((b, d), lambda i, j: (0, 0)),
                  pl.BlockSpec((1, d, tn), lambda i, j: (i, 0, j)),
                  pl.BlockSpec((1, 1, tn), lambda i, j: (i, 0, j))],
        out_specs=pl.BlockSpec((1, b, tn), lambda i, j: (i, 0, j)),
        out_shape=jax.ShapeDtypeStruct((depth, b, n), F32),
        name="ada_mod",
        compiler_params=_params("arbitrary", "arbitrary"),
    )(c, ada_w, ada_b.reshape(depth, 1, n))


def _ffn_body(final_norm, mixer_out, nf, h_ref, sh_ref, sc_ref, g_ref, gain_ref, wup_ref, cw_ref, cb_ref,
              wdn_ref, gfin_ref, *rest):
    if mixer_out:
        gm_ref, y_ref, yg_ref, wo_ref, o_ref, carry_ref, hn_ref, act_ref, res_ref = rest
    else:
        o_ref, carry_ref, hn_ref, act_ref, res_ref = rest
    tm = h_ref.shape[1]
    tf = wup_ref.shape[3]

    @pl.when(pl.program_id(1) == 0)
    def _():
        carry_ref[...] = jnp.zeros_like(carry_ref)

    if mixer_out:
        y = jnp.concatenate([y_ref[0, p] for p in range(y_ref.shape[1])], axis=1)
        res_ref[...] = h_ref[0] + gm_ref[0] * _dot(y * yg_ref[0], wo_ref[...])
    else:
        res_ref[...] = h_ref[0]
    hn_ref[...] = _rms_mod(res_ref[...], gain_ref[...], sc_ref[0], sh_ref[0]).astype(BF16)
    rid = lax.broadcasted_iota(I32, (tm, 1), 0)

    def conv(part, j):
        u = _dot(hn_ref[...], wup_ref[part, j])
        prev = carry_ref[part, j]
        carry_ref[part, j] = u[tm - SUBLANES:, :]
        p1 = prev[SUBLANES - 1:SUBLANES, :]
        p2 = prev[SUBLANES - 2:SUBLANES - 1, :]
        s1 = jnp.where(rid == 0, p1, pltpu.roll(u, 1, 0))
        s2 = jnp.where(rid == 0, p2, jnp.where(rid == 1, p1, pltpu.roll(u, 2, 0)))
        cw = cw_ref[part, j]
        return cw[2:3] * u + cw[1:2] * s1 + cw[0:1] * s2 + cb_ref[part, j]

    for j in range(nf):
        ug = conv(0, j)
        uv = conv(1, j)
        act_ref[:, j * tf:(j + 1) * tf] = (ug * _sigmoid(ug) * uv).astype(BF16)
    out = res_ref[...] + g_ref[0] * _dot(act_ref[...], wdn_ref[...])
    if final_norm:
        out = _rms(out, gfin_ref[...])
    o_ref[0] = out


def _ffn(h, sh, sc, g, gain, w_up, conv_w, conv_b, w_down, gfin, final_norm, mixer=None, tm=512, tf=256):
    b, s, d = h.shape
    f = w_down.shape[0]
    nf = f // tf
    tm = min(tm, s)
    wup = w_up.astype(BF16).reshape(d, 2, nf, tf).transpose(1, 2, 0, 3)
    cw = conv_w.reshape(CONV_W, 2, nf, tf).transpose(1, 2, 0, 3)
    cb = conv_b.reshape(2, nf, 1, tf)
    wdn = w_down.astype(BF16)
    row = pl.BlockSpec((1, 1, d), lambda bi, i: (bi, 0, 0))
    tile = pl.BlockSpec((1, tm, d), lambda bi, i: (bi, i, 0))
    in_specs = [tile, row, row, row, _full((1, d)), _full(wup.shape, True), _full(cw.shape),
                _full(cb.shape), _full(wdn.shape, True), _full((1, d))]
    args = [h, sh, sc, g, gain.reshape(1, d), wup, cw, cb, wdn, gfin.reshape(1, d)]
    if mixer is not None:
        gm, y, yg, w_o = mixer
        in_specs += [row, pl.BlockSpec((1, d // LANES, tm, LANES), lambda bi, i: (bi, 0, i, 0)), tile,
                     _full((d, d), True)]
        args += [gm, y, yg, w_o.astype(BF16)]
    return pl.pallas_call(
        functools.partial(_ffn_body, final_norm, mixer is not None, nf),
        grid=(b, s // tm),
        in_specs=in_specs,
        out_specs=tile,
        out_shape=jax.ShapeDtypeStruct((b, s, d), F32),
        scratch_shapes=[pltpu.VMEM((2, nf, SUBLANES, tf), F32),
                        pltpu.VMEM((tm, d), BF16),
                        pltpu.VMEM((tm, f), BF16),
                        pltpu.VMEM((tm, d), F32)],
        name="conv_ffn",
        compiler_params=_params("arbitrary", "arbitrary"),
    )(*args)


def _fold_body(wuq_ref, wuk_ref, wuv_ref, wo_ref, wq_ref, wvo_ref):
    hi = lax.Precision.HIGHEST
    wq_ref[0] = lax.dot_general(wuq_ref[0], wuk_ref[0], NT, precision=hi,
                                preferred_element_type=F32) * (A_HEAD_DIM ** -0.5 * LOG2E)
    wvo_ref[0] = jnp.dot(wuv_ref[0], wo_ref[0], precision=hi, preferred_element_type=F32)


def _fold(w_uq, w_uk, w_uv, w_o):
    rq = w_uq.shape[0]
    h, c, dh = w_uk.shape
    d = w_o.shape[1]
    wuq = w_uq.reshape(rq, h, dh).transpose(1, 0, 2)
    wo = w_o.reshape(h, dh, d)
    per_head = lambda *shape: pl.BlockSpec((1,) + shape, lambda i: (i, 0, 0))
    wq, wvo = pl.pallas_call(
        _fold_body,
        grid=(h,),
        in_specs=[per_head(rq, dh), per_head(c, dh), per_head(c, dh), per_head(dh, d)],
        out_specs=[per_head(rq, c), per_head(c, d)],
        out_shape=[jax.ShapeDtypeStruct((h, rq, c), F32), jax.ShapeDtypeStruct((h, c, d), F32)],
        name="dsa_fold",
        compiler_params=_params("arbitrary"),
    )(wuq, w_uk, w_uv, wo)
    return (wq.transpose(1, 0, 2).reshape(rq, h * c).astype(BF16),
            wvo.reshape(h * c, d).astype(BF16))


def _dsa_proj_body(x_ref, sh_ref, sc_ref, gain_ref, win_ref, qn_ref, kvn_ref, kg_ref, kb_ref,
                   wq_ref, wqi_ref, qabs_ref, qidx_ref, aux_ref, kpad_ref, caug_ref):
    tm = x_ref.shape[1]
    hn = _rms_mod(x_ref[0], gain_ref[...], sc_ref[0], sh_ref[0]).astype(BF16)
    proj = _dot(hn, win_ref[...])
    ql = _rms(proj[:, :A_Q_RANK], qn_ref[...]).astype(BF16)
    ck = _rms(proj[:, A_Q_RANK:A_Q_RANK + A_KV_RANK], kvn_ref[...])
    slab = proj[:, A_Q_RANK + A_KV_RANK:]
    lane = lax.broadcasted_iota(I32, (1, LANES), 1)
    is_k = lane < IDX_DIM
    mu = jnp.sum(jnp.where(is_k, slab, 0.0), axis=-1, keepdims=True) * (1.0 / IDX_DIM)
    dk = jnp.where(is_k, slab - mu, 0.0)
    var = jnp.sum(dk * dk, axis=-1, keepdims=True) * (1.0 / IDX_DIM)
    kpad_ref[0] = (dk * lax.rsqrt(var + EPS) * kg_ref[...] + kb_ref[...]).astype(BF16)
    aux_ref[0] = slab * (IDX_HEADS ** -0.5 * IDX_DIM ** -0.5)
    qabs_ref[0] = _dot(ql, wq_ref[...]).astype(BF16)
    qidx_ref[0] = _dot(ql, wqi_ref[...]).astype(BF16)
    pos = pl.program_id(1) * tm + lax.broadcasted_iota(I32, (tm, 1), 0)
    p_hi = (pos >> 6).astype(F32)
    p_lo = (pos & 63).astype(F32)
    extra = jnp.where(lane < 2, p_hi, jnp.where(lane < 4, p_lo, 0.0))
    caug_ref[0, :, :A_KV_RANK] = ck.astype(BF16)
    caug_ref[0, :, A_KV_RANK:] = extra.astype(BF16)


def _dsa_proj(x, sh, sc, gain, w_in, q_norm, kv_norm, kidx_g, kidx_b, wq, w_qi, tm=512):
    b, s, d = x.shape
    tm = min(tm, s)
    n_in = w_in.shape[1]
    win = jnp.pad(w_in, ((0, 0), (0, 4 * LANES - n_in))).astype(BF16)
    wqi = jnp.pad(w_qi.reshape(A_Q_RANK, IDX_HEADS, IDX_DIM),
                  ((0, 0), (0, 0), (0, LANES - IDX_DIM))).reshape(A_Q_RANK, IDX_HEADS * LANES).astype(BF16)
    kg = jnp.pad(kidx_g, (0, LANES - IDX_DIM)).reshape(1, LANES)
    kb = jnp.pad(kidx_b, (0, LANES - IDX_DIM)).reshape(1, LANES)
    row = pl.BlockSpec((1, 1, d), lambda bi, i: (bi, 0, 0))
    tile = lambda n: pl.BlockSpec((1, tm, n), lambda bi, i: (bi, i, 0))
    nq, ni = A_HEADS * A_KV_RANK, IDX_HEADS * LANES
    return pl.pallas_call(
        _dsa_proj_body,
        grid=(b, s // tm),
        in_specs=[tile(d), row, row, _full((1, d)), _full(win.shape), _full((1, A_Q_RANK)),
                  _full((1, A_KV_RANK)), _full((1, LANES)), _full((1, LANES)),
                  _full(wq.shape), _full(wqi.shape)],
        out_specs=[tile(nq), tile(ni), tile(LANES), tile(LANES), tile(2 * A_KV_RANK)],
        out_shape=[jax.ShapeDtypeStruct((b, s, nq), BF16), jax.ShapeDtypeStruct((b, s, ni), BF16),
                   jax.ShapeDtypeStruct((b, s, LANES), F32), jax.ShapeDtypeStruct((b, s, LANES), BF16),
                   jax.ShapeDtypeStruct((b, s, 2 * A_KV_RANK), BF16)],
        name="dsa_proj",
        compiler_params=_params("arbitrary", "arbitrary"),
    )(x, sh, sc, gain.reshape(1, d), win, q_norm.reshape(1, -1), kv_norm.reshape(1, -1), kg, kb, wq, wqi)


def _bf16_digits(v):
    hi = float(np.asarray(v, np.float32).astype(BF16).astype(np.float32))
    lo = float(np.asarray(v - hi, np.float32).astype(BF16).astype(np.float32))
    return hi, lo


def _dsa_attn_body(x_ref, gm_ref, qabs_ref, qidx_ref, aux_ref, kpad_ref, caug_ref, wvo_ref, o_ref,
                   keys_ref, keyst_ref, qaug_ref, qis_ref, wb_ref, bias_ref, ahead_ref, t_ref, m_ref, al_ref,
                   l_ref, acc_ref, s_ref, p_ref):
    tq = x_ref.shape[1]
    seq = kpad_ref.shape[1]
    c = A_KV_RANK
    nj = KB // LANES
    q0 = pl.program_id(1) * tq
    kend = q0 + tq
    n_kb = (kend + KB - 1) // KB
    tail = (n_kb - 1) * KB
    lane = lax.broadcasted_iota(I32, (1, LANES), 1)
    trow = lax.broadcasted_iota(I32, (tq, 1), 0)
    slopes = [LOG2E * 2.0 ** (-8.0 * (h + 1) / A_HEADS) for h in range(A_HEADS)]

    for h in range(A_HEADS):
        rows = slice(h * tq, (h + 1) * tq)
        hi, lo = _bf16_digits(slopes[h])
        digits = jnp.where(lane == 0, 64.0 * hi, jnp.where(lane == 1, 64.0 * lo,
                           jnp.where(lane == 2, hi, jnp.where(lane == 3, lo, 0.0))))
        qaug_ref[rows, :c] = qabs_ref[0, :, h * c:(h + 1) * c]
        qaug_ref[rows, c:] = jnp.broadcast_to(digits, (tq, LANES)).astype(BF16)
    aux = aux_ref[0]
    for h in range(IDX_HEADS):
        qis_ref[h * tq:(h + 1) * tq, :] = qidx_ref[0, :, h * LANES:(h + 1) * LANES]
        wb_ref[h] = jnp.broadcast_to(aux[:, IDX_DIM + h:IDX_DIM + h + 1], (tq, LANES))

    qchunk = (q0 + trow) >> 6

    def to_key(sc):
        bits = pltpu.bitcast(sc, I32)
        return bits ^ ((bits >> 31) & 0x7FFFFFFF)

    def score_block(kb, carry):
        off = pl.multiple_of(kb * KB, KB)
        z = _dot_nt(qis_ref[...], kpad_ref[0, pl.ds(off, KB), :])
        for j in range(nj):
            sc = jnp.zeros((tq, LANES), F32)
            for h in range(IDX_HEADS):
                sc = sc + wb_ref[h] * jnp.maximum(z[h * tq:(h + 1) * tq, j * LANES:(j + 1) * LANES], 0.0)
            kchunk = (off + j * LANES + lane) >> 6
            sc = jnp.where(kchunk <= qchunk, sc, -jnp.inf)
            keys_ref[kb * nj + j] = to_key(sc)
            keyst_ref[pl.ds(pl.multiple_of(off + j * LANES, LANES), LANES), :] = to_key(sc.T)
        return carry

    lax.fori_loop(0, n_kb, score_block, 0)

    slab = 8 * SUBLANES
    krow = lax.broadcasted_iota(I32, (slab, LANES), 0)

    def count(pred):
        def body(kb, acc):
            for u in range(KB // slab):
                r0 = pl.multiple_of(kb * KB + u * slab, slab)
                acc = jnp.where(pred(keyst_ref[pl.ds(r0, slab), :], r0 + krow), acc + 1.0, acc)
            return acc
        part = lax.fori_loop(0, n_kb, body, jnp.zeros((slab, LANES), F32))
        return jnp.sum(part, axis=0, keepdims=True)

    def per_row(v):
        return pltpu.bitcast(pltpu.bitcast(jnp.broadcast_to(v, (LANES, LANES)), F32).T, I32)

    t_ref[...] = jnp.full(t_ref.shape, INT_MIN, I32)

    @pl.when(kend > TOPK_MAX)
    def _():
        kf = float(TOPK_MAX)
        n_pos = count(lambda k, s: k >= 0)
        t0 = jnp.where(n_pos >= kf, 0, INT_MIN).astype(I32)

        def bit_step(i, t):
            cand = t | jnp.left_shift(jnp.int32(1), 30 - i)
            return jnp.where(count(lambda k, s: k >= cand) >= kf, cand, t)

        t = lax.fori_loop(0, 31, bit_step, t0)
        t_ref[...] = per_row(t)
        n_ge = count(lambda k, s: k >= t)

        @pl.when(jnp.max(n_ge) > kf)
        def _():
            need = kf - count(lambda k, s: k > t)
            n_bits = max(1, int(seq - 1).bit_length())

            def idx_step(i, x):
                cand = x | jnp.left_shift(jnp.int32(1), n_bits - 1 - i)
                below = count(lambda k, s: (k == t) & (s < cand))
                return jnp.where(below < need, cand, x)

            last = per_row(lax.fori_loop(0, n_bits, idx_step, jnp.zeros((1, LANES), I32)))
            thr_ = t_ref[...]
            demoted = jnp.where(thr_ == INT_MIN, thr_, thr_ - 1)

            def demote(j, carry):
                k = keys_ref[j]
                keys_ref[j] = jnp.where((k == thr_) & (j * LANES + lane > last), demoted, k)
                return carry

            lax.fori_loop(0, n_kb * nj, demote, 0)

    m_ref[...] = jnp.full(m_ref.shape, NEG, F32)
    l_ref[...] = jnp.zeros_like(l_ref)
    acc_ref[...] = jnp.zeros_like(acc_ref)
    thr = t_ref[...]
    hg = A_HEADS // ATT_GROUPS
    group = lambda g: slice(g * hg * tq, (g + 1) * hg * tq)

    def attend(kb, slot, is_tail):
        cb = caug_ref[0, pl.ds(pl.multiple_of(kb * KB, KB), KB), :]
        for g in range(ATT_GROUPS):
            s_ref[slot, group(g), :] = _dot_nt(qaug_ref[group(g), :], cb)
        for j in range(nj):
            cols = slice(j * LANES, (j + 1) * LANES)
            picked = keys_ref[kb * nj + j] >= thr
            if is_tail:
                kcol = tail + j * LANES + lane
                picked = picked & ((kcol >> 6) <= ((q0 + trow) >> 6))
                ahead_ref[:, cols] = -2.0 * jnp.maximum(kcol - (q0 + trow), 0).astype(F32)
            bias_ref[slot, :, cols] = jnp.where(picked, 0.0, NEG)
        for g in range(ATT_GROUPS):
            for h in range(g * hg, (g + 1) * hg):
                rows = slice(h * tq, (h + 1) * tq)
                tiles = []
                for j in range(nj):
                    cols = slice(j * LANES, (j + 1) * LANES)
                    t_ = s_ref[slot, rows, cols] + bias_ref[slot, :, cols]
                    tiles.append(t_ + slopes[h] * ahead_ref[:, cols] if is_tail else t_)
                top = functools.reduce(jnp.maximum, tiles)
                m_old = m_ref[rows, :]
                m_new = jnp.maximum(m_old, jnp.broadcast_to(jnp.max(top, axis=-1, keepdims=True), top.shape))
                ps = [jnp.exp2(t_ - m_new) for t_ in tiles]
                p_ref[slot, rows, :] = jnp.concatenate(ps, axis=1).astype(BF16)
                alpha = jnp.exp2(m_old - m_new)
                l_ref[rows, :] = l_ref[rows, :] * alpha + functools.reduce(jnp.add, ps)
                al_ref[rows, :] = alpha
                m_ref[rows, :] = m_new
            acc_ref[group(g), :] = (acc_ref[group(g), :] * al_ref[group(g), :]
                                    + _dot(p_ref[slot, group(g), :], cb[:, :c]))

    n_past = n_kb - 1

    def past_pair(i, carry):
        attend(2 * i, 0, False)
        attend(2 * i + 1, 1, False)
        return carry

    lax.fori_loop(0, n_past // 2, past_pair, 0)

    @pl.when(n_past % 2 == 1)
    def _():
        attend(n_past - 1, 0, False)

    attend(n_past, 1, True)

    y = jnp.zeros((tq, x_ref.shape[2]), F32)
    for pair in range(A_HEADS // 2):
        parts = []
        for h in (2 * pair, 2 * pair + 1):
            rows = slice(h * tq, (h + 1) * tq)
            parts.append((acc_ref[rows, :] / jnp.sum(l_ref[rows, :], axis=-1, keepdims=True)).astype(BF16))
        y = y + _dot(jnp.concatenate(parts, axis=1), wvo_ref[2 * pair * c:(2 * pair + 2) * c, :])
    o_ref[0] = x_ref[0] + gm_ref[0] * y


def _dsa_attn(x, gm, qabs, qidx, aux, kpad, caug, wvo):
    b, s, d = x.shape
    tq = TQ
    assert tq == LANES and s % KB == 0
    hq = A_HEADS * tq
    row = pl.BlockSpec((1, 1, d), lambda bi, i: (bi, 0, 0))
    tile = lambda n: pl.BlockSpec((1, tq, n), lambda bi, i: (bi, i, 0))
    per_batch = lambda n: pl.BlockSpec((1, s, n), lambda bi, i: (bi, 0, 0))
    return pl.pallas_call(
        _dsa_attn_body,
        grid=(b, s // tq),
        in_specs=[tile(d), row, tile(qabs.shape[2]), tile(qidx.shape[2]), tile(LANES),
                  per_batch(LANES), per_batch(2 * A_KV_RANK), _full(wvo.shape)],
        out_specs=tile(d),
        out_shape=jax.ShapeDtypeStruct((b, s, d), F32),
        scratch_shapes=[pltpu.VMEM((s // LANES, tq, LANES), I32),
                        pltpu.VMEM((s, tq), I32),
                        pltpu.VMEM((hq, 2 * A_KV_RANK), BF16),
                        pltpu.VMEM((IDX_HEADS * tq, LANES), BF16),
                        pltpu.VMEM((IDX_HEADS, tq, LANES), F32),
                        pltpu.VMEM((2, tq, KB), F32),
                        pltpu.VMEM((tq, KB), F32),
                        pltpu.VMEM((tq, LANES), I32),
                        pltpu.VMEM((hq, LANES), F32),
                        pltpu.VMEM((hq, LANES), F32),
                        pltpu.VMEM((hq, LANES), F32),
                        pltpu.VMEM((hq, A_KV_RANK), F32),
                        pltpu.VMEM((2, hq, KB), F32),
                        pltpu.VMEM((2, hq, KB), BF16)],
        name="dsa_attn",
        compiler_params=_params("arbitrary", "arbitrary"),
    )(x, gm, qabs, qidx, aux, kpad, caug, wvo)


def _rwkv_proj_body(h_ref, sh_ref, sc_ref, gain_ref, mu_ref, wr_ref, wk_ref, wv_ref, w1_ref, w2_ref,
                    a1_ref, a2_ref, g1_ref, g2_ref, w0_ref, a0_ref,
                    r_ref, k_ref, v_ref, lw_ref, a_ref, g_ref, carry_ref):
    tm = h_ref.shape[1]

    @pl.when(pl.program_id(1) == 0)
    def _():
        carry_ref[...] = jnp.zeros_like(carry_ref)

    hn = _rms_mod(h_ref[0], gain_ref[...], sc_ref[0], sh_ref[0])
    prev = carry_ref[SUBLANES - 1:SUBLANES, :]
    carry_ref[...] = hn[tm - SUBLANES:, :]
    rid = lax.broadcasted_iota(I32, (tm, 1), 0)
    dx = jnp.where(rid == 0, prev, pltpu.roll(hn, 1, 0)) - hn
    mix = lambda j: (hn + dx * mu_ref[j:j + 1, :]).astype(BF16)

    def put(ref, val):
        for p in range(val.shape[1] // LANES):
            ref[0, p] = val[:, p * LANES:(p + 1) * LANES].astype(ref.dtype)

    put(r_ref, _dot(mix(0), wr_ref[...]))
    put(k_ref, _dot(mix(2), wk_ref[...]))
    put(v_ref, _dot(mix(3), wv_ref[...]))
    ww = w0_ref[...] + _dot(jnp.tanh(_dot(mix(1), w1_ref[...])).astype(BF16), w2_ref[...])
    softplus = jnp.maximum(-ww, 0.0) + jnp.log(1.0 + jnp.exp(-jnp.abs(ww)))
    put(lw_ref, -jnp.exp(-softplus - 0.5))
    put(a_ref, _sigmoid(a0_ref[...] + _dot(_dot(mix(4), a1_ref[...]).astype(BF16), a2_ref[...])))
    g_ref[0] = _dot(_sigmoid(_dot(mix(5), g1_ref[...])).astype(BF16), g2_ref[...]).astype(g_ref.dtype)


def _pad_rank(w_down, w_up):
    r = w_down.shape[1]
    p = -r % LANES
    return (jnp.pad(w_down, ((0, 0), (0, p))).astype(BF16), jnp.pad(w_up, ((0, p), (0, 0))).astype(BF16))


def _rwkv_proj(h, sh, sc, gain, mu, w_r, w_k, w_v, w0, w1, w2, a0, a1, a2, g1, g2, tm=512):
    b, s, d = h.shape
    tm = min(tm, s)
    w1p, w2p = _pad_rank(w1, w2)
    a1p, a2p = _pad_rank(a1, a2)
    g1p, g2p = _pad_rank(g1, g2)
    row = pl.BlockSpec((1, 1, d), lambda bi, i: (bi, 0, 0))
    tile = pl.BlockSpec((1, tm, d), lambda bi, i: (bi, i, 0))
    big = [w.astype(BF16) for w in (w_r, w_k, w_v)]
    small = [w1p, w2p, a1p, a2p, g1p, g2p]
    np_ = d // LANES
    ptile = pl.BlockSpec((1, np_, tm, LANES), lambda bi, i: (bi, 0, i, 0))
    pout = lambda dt: jax.ShapeDtypeStruct((b, np_, s, LANES), dt)
    return pl.pallas_call(
        _rwkv_proj_body,
        grid=(b, s // tm),
        in_specs=[tile, row, row, _full((1, d)), _full(mu.shape)] + [_full(w.shape) for w in big + small]
                 + [_full((1, d)), _full((1, d))],
        out_specs=[ptile] * 5 + [tile],
        out_shape=[pout(BF16), pout(BF16), pout(BF16), pout(F32), pout(BF16),
                   jax.ShapeDtypeStruct((b, s, d), BF16)],
        scratch_shapes=[pltpu.VMEM((SUBLANES, d), F32)],
        name="rwkv_proj",
        compiler_params=_params("arbitrary", "arbitrary"),
    )(h, sh, sc, gain.reshape(1, d), mu, *big, *small, w0.reshape(1, d), a0.reshape(1, d))


def _rwkv_scan_body(n_chunks, r_ref, k_ref, v_ref, lw_ref, a_ref, kk_ref, ka_ref, rk_ref, gg_ref, gb_ref,
                    y_ref, st_ref, w_s, rt_s, u0_s, y0_s, bon_s, arb_s, bkt_s, plt_s):
    L = SCAN_L
    P2 = 2 * L
    n_pairs = r_ref.shape[1]

    @pl.when(pl.program_id(1) == 0)
    def _():
        st_ref[...] = jnp.zeros_like(st_ref)

    lane = lax.broadcasted_iota(I32, (1, LANES), 1)
    head0 = lane < B_HEAD
    m0 = jnp.where(head0, 1.0, 0.0)
    m1 = 1.0 - m0
    ri = lax.broadcasted_iota(I32, (P2, P2), 0)
    ci = lax.broadcasted_iota(I32, (P2, P2), 1)
    same = (ri >> 6) == (ci >> 6)
    strict = same & ((ci & 63) < (ri & 63))
    incl = same & ((ci & 63) <= (ri & 63))
    eye = jnp.where(ri == ci, 1.0, 0.0)
    tril = jnp.where(lax.broadcasted_iota(I32, (L, L), 1) <= lax.broadcasted_iota(I32, (L, L), 0),
                     1.0, 0.0).astype(BF16)
    def seg_sum(x):
        s0 = jnp.sum(x * m0, axis=-1, keepdims=True)
        s1 = jnp.sum(x * m1, axis=-1, keepdims=True)
        return jnp.where(head0, s0, s1)

    stack = lambda x: jnp.concatenate([x * m0, x * m1], axis=0)
    fold = lambda x: x[:L] + x[L:]

    bdot = lambda a, b: _dot(a.astype(BF16), b.astype(BF16))

    each = lambda f, *xs: [f(*a) for a in zip(*xs)]

    def prepare_pair(p, carry):
        k_k, k_a, r_k = kk_ref[p], ka_ref[p], rk_ref[p]
        cs = range(n_chunks)
        r, kraw, v, lw, asig = ([ref[0, p, c * L:(c + 1) * L, :].astype(F32) for c in cs]
                                for ref in (r_ref, k_ref, v_ref, lw_ref, a_ref))
        kk = each(lambda x: x * k_k, kraw)
        kkn = each(lambda x: x / jnp.maximum(jnp.sqrt(seg_sum(x * x)), 1e-12), kk)
        kmod = each(lambda x, a: x * (1.0 + (a - 1.0) * k_a), kraw, asig)
        bvec = each(jnp.multiply, kkn, asig)
        gc = each(lambda x: _dot(tril, jnp.concatenate(_split(x), axis=1)), lw)
        g = each(lambda x: x[:, :LANES] + x[:, LANES:], gc)
        g_last = each(lambda x: x[L - 1:L, :], g)
        at = each(lambda kn, x, w: -kn * jnp.exp(x - w), kkn, g, lw)
        rt = each(lambda x, y: x * jnp.exp(y), r, g)
        inv = each(lambda x: jnp.exp(-x), g)
        to_end = each(lambda x, y: jnp.exp(x - y), g_last, g)
        bk_end = each(lambda b_, k_, e: jnp.concatenate([b_ * e, k_ * e], axis=0), bvec, kmod, to_end)
        lhs = each(lambda a, b_: jnp.concatenate([stack(a), stack(b_)], axis=0).astype(BF16), at, rt)
        rhs = each(lambda b_, k_, i: jnp.concatenate([b_ * i, b_ * i, k_ * i, k_ * i], axis=0).astype(BF16),
                   bvec, kmod, inv)
        mm = each(_dot_nt, lhs, rhs)
        n_ab = each(lambda x: jnp.where(strict, x[:P2, :P2], 0.0), mm)
        a_kk = each(lambda x: jnp.concatenate([jnp.where(strict, x[:P2, P2:], 0.0),
                                               jnp.where(incl, x[P2:, P2:], 0.0)], axis=0), mm)
        a_rb = each(lambda x: jnp.where(incl, x[P2:, :P2], 0.0), mm)

        tmat = each(lambda x: eye + x, n_ab)
        pw = n_ab
        for _ in range(int(L).bit_length() - 2):
            pw = each(bdot, pw, pw)
            tmat = each(lambda t, q: t + bdot(t, q), tmat, pw)

        akv = each(lambda a, x: bdot(a, stack(x)), a_kk, v)
        tw = each(lambda t, a, x: bdot(t, jnp.concatenate([stack(a), stack(fold(x[:P2]))], axis=1)),
                  tmat, at, akv)
        for c in cs:
            w_s[p, c] = fold(tw[c][:, :LANES]).astype(BF16)
            rt_s[p, c] = rt[c].astype(BF16)
            u0_s[p, c] = fold(tw[c][:, LANES:])
            y0_s[p, c] = fold(akv[c][P2:])
            bon_s[p, c] = seg_sum(r[c] * kmod[c] * r_k) * v[c]
            arb_s[p, c] = a_rb[c].astype(BF16)
            bkt_s[p, c] = bk_end[c].T.astype(BF16)
            plt_s[p, c] = jnp.broadcast_to(jnp.exp(g_last[c]), (P2, LANES)).T
        return carry

    lax.fori_loop(0, n_pairs, prepare_pair, 0)

    def advance(c, carry):
        rows = pl.ds(pl.multiple_of(c * L, L), L)
        ps = range(n_pairs)
        st = [st_ref[p] for p in ps]
        ws = [_dot(jnp.concatenate([w_s[p, c], rt_s[p, c]], axis=0), st[p].astype(BF16)) for p in ps]
        u = [ws[p][:L] + u0_s[p, c] for p in ps]
        uv = [jnp.concatenate([u[p].astype(BF16), v_ref[0, p, rows, :]], axis=0) for p in ps]
        upd = [_dot(bkt_s[p, c], uv[p]) for p in ps]
        for p in ps:
            st_ref[p] = st[p] * plt_s[p, c] + jnp.where(same, upd[p], 0.0)
        au = [_dot(arb_s[p, c], stack(u[p]).astype(BF16)) for p in ps]
        for p in ps:
            y = ws[p][L:] + fold(au[p]) + y0_s[p, c]
            mu = seg_sum(y) * (1.0 / B_HEAD)
            dy = y - mu
            var = seg_sum(dy * dy) * (1.0 / B_HEAD)
            y_ref[0, p, rows, :] = (dy * lax.rsqrt(var + GN_EPS) * gg_ref[p] + gb_ref[p]
                                    + bon_s[p, c]).astype(y_ref.dtype)
        return carry

    lax.fori_loop(0, n_chunks, advance, 0)


def _rwkv_scan(r, k, v, lw, a, k_k, k_a, r_k, gn_g, gn_b, lb=512):
    b, n_pairs, s, _ = r.shape
    lb = min(lb, s)
    nc = lb // SCAN_L
    tile = pl.BlockSpec((1, n_pairs, lb, LANES), lambda bi, i: (bi, 0, i, 0))
    vec = lambda x: x.reshape(n_pairs, 1, LANES)
    per_chunk = lambda rows, dt: pltpu.VMEM((n_pairs, nc, rows, LANES), dt)
    return pl.pallas_call(
        functools.partial(_rwkv_scan_body, nc),
        grid=(b, s // lb),
        in_specs=[tile] * 5 + [_full((n_pairs, 1, LANES))] * 5,
        out_specs=tile,
        out_shape=jax.ShapeDtypeStruct(r.shape, BF16),
        scratch_shapes=[pltpu.VMEM((n_pairs, LANES, LANES), F32),
                        per_chunk(SCAN_L, BF16), per_chunk(SCAN_L, BF16),
                        per_chunk(SCAN_L, F32), per_chunk(SCAN_L, F32), per_chunk(SCAN_L, F32),
                        per_chunk(2 * SCAN_L, BF16), per_chunk(2 * SCAN_L, BF16),
                        per_chunk(2 * SCAN_L, F32)],
        name="rwkv_scan",
        compiler_params=_params("arbitrary", "arbitrary"),
    )(r, k, v, lw, a, vec(k_k), vec(k_a), vec(r_k), vec(gn_g), vec(gn_b))


def kernel(x, c, ada_w, ada_b, norm_mix, norm_ffn, norm_final, a_w_in, a_q_norm, a_kv_norm, a_kidx_g, a_kidx_b, a_w_uq, a_w_qi, a_w_uk, a_w_uv, a_w_o, b_mu, b_w_r, b_w_k, b_w_v, b_w_o, b_w0, b_w1, b_w2, b_a0, b_a1, b_a2, b_g1, b_g2, b_k_k, b_k_a, b_r_k, b_gn_g, b_gn_b, f_w_up, f_conv_w, f_conv_b, f_w_down):
    b, s, d = x.shape
    mod = _ada(c, ada_w, ada_b).reshape(ada_w.shape[0], b, 6, 1, d)
    sh_m, sc_m, g_m, sh_f, sc_f, g_f = (mod[:, :, j] for j in range(6))

    wq, wvo = _fold(a_w_uq[0], a_w_uk[0], a_w_uv[0], a_w_o[0])
    qabs, qidx, aux, kpad, caug = _dsa_proj(x, sh_m[0], sc_m[0], norm_mix[0], a_w_in[0], a_q_norm[0],
                                            a_kv_norm[0], a_kidx_g[0], a_kidx_b[0], wq, a_w_qi[0])
    h = _dsa_attn(x, g_m[0], qabs, qidx, aux, kpad, caug, wvo)
    h = _ffn(h, sh_f[0], sc_f[0], g_f[0], norm_ffn[0], f_w_up[0], f_conv_w[0], f_conv_b[0], f_w_down[0],
             norm_final, False)

    r, k, v, lw, a, g = _rwkv_proj(h, sh_m[1], sc_m[1], norm_mix[1], b_mu[0], b_w_r[0], b_w_k[0], b_w_v[0],
                                   b_w0[0], b_w1[0], b_w2[0], b_a0[0], b_a1[0], b_a2[0], b_g1[0], b_g2[0])
    y = _rwkv_scan(r, k, v, lw, a, b_k_k[0], b_k_a[0], b_r_k[0].reshape(-1), b_gn_g[0], b_gn_b[0])
    return _ffn(h, sh_f[1], sc_f[1], g_f[1], norm_ffn[1], f_w_up[1], f_conv_w[1], f_conv_b[1], f_w_down[1],
                norm_final, True, mixer=(g_m[1], y, g, b_w_o[0]))
```

```python
import functools

import numpy as np
import jax
import jax.numpy as jnp
from jax import lax
from jax.experimental import pallas as pl
from jax.experimental.pallas import tpu as pltpu

F32 = jnp.float32
BF16 = jnp.bfloat16
I32 = jnp.int32

EPS = 1e-6
CHUNK = 64
A_HEADS = 16
A_HEAD_DIM = 64
A_Q_RANK = 256
A_KV_RANK = 128
IDX_HEADS = 8
IDX_DIM = 64
TOPK_MAX = 256
B_HEAD = 64
GN_EPS = 64e-5
CONV_W = 3

LANES = 128
SUBLANES = 8
VMEM_LIMIT = 56 * 1024 * 1024
NEG = -1e30
INT_MIN = -(2 ** 31)

LOG2E = 1.4426950408889634
TQ = 128
KB = 512
ATT_GROUPS = 2
SCAN_L = 64
NT = (((1,), (1,)), ((), ()))


def _dot(a, b):
    return jnp.dot(a, b, preferred_element_type=F32)


def _dot_nt(a, b):
    return lax.dot_general(a, b, NT, preferred_element_type=F32)


def _split(x):
    hi = x.astype(BF16)
    lo = (x - hi.astype(F32)).astype(BF16)
    return hi, lo


def _sigmoid(x):
    return 1.0 / (1.0 + jnp.exp(-x))


def _rms(x, gain):
    return x * lax.rsqrt(jnp.mean(x * x, axis=-1, keepdims=True) + EPS) * gain


def _rms_mod(x, gain, sc, sh):
    return _rms(x, gain) * (1.0 + sc) + sh


def _params(*sem):
    return pltpu.CompilerParams(dimension_semantics=sem, vmem_limit_bytes=VMEM_LIMIT)


def _full(shape, single=False):
    n = len(shape)
    mode = dict(pipeline_mode=pl.Buffered(1)) if single else {}
    return pl.BlockSpec(shape, lambda *_: (0,) * n, **mode)


def _ada_body(c_ref, w_ref, b_ref, o_ref):
    c = c_ref[...]
    o_ref[0] = _dot(c * _sigmoid(c), w_ref[0]) + b_ref[0]


def _ada(c, ada_w, ada_b):
    depth, d, n = ada_w.shape
    b = c.shape[0]
    tn = n // 4
    return pl.pallas_call(
        _ada_body,
        grid=(depth, n // tn),
        in_specs=[pl.BlockSpec((b, d), lambda i, j: (0, 0)),
                  pl.BlockSpec((1, d, tn), lambda i, j: (i, 0, j)),
                  pl.BlockSpec((1, 1, tn), lambda i, j: (i, 0, j))],
        out_specs=pl.BlockSpec((1, b, tn), lambda i, j: (i, 0, j)),
        out_shape=jax.ShapeDtypeStruct((depth, b, n), F32),
        name="ada_mod",
        compiler_params=_params("arbitrary", "arbitrary"),
    )(c, ada_w, ada_b.reshape(depth, 1, n))


def _ffn_body(final_norm, mixer_out, nf, h_ref, sh_ref, sc_ref, g_ref, gain_ref, wup_ref, cw_ref, cb_ref,
              wdn_ref, gfin_ref, *rest):
    if mixer_out:
        gm_ref, y_ref, yg_ref, wo_ref, o_ref, carry_ref, hn_ref, act_ref, res_ref = rest
    else:
        o_ref, carry_ref, hn_ref, act_ref, res_ref = rest
    tm = h_ref.shape[1]
    tf = wup_ref.shape[3]

    @pl.when(pl.program_id(1) == 0)
    def _():
        carry_ref[...] = jnp.zeros_like(carry_ref)

    if mixer_out:
        y = jnp.concatenate([y_ref[0, p] for p in range(y_ref.shape[1])], axis=1)
        res_ref[...] = h_ref[0] + gm_ref[0] * _dot(y * yg_ref[0], wo_ref[...])
    else:
        res_ref[...] = h_ref[0]
    hn_ref[...] = _rms_mod(res_ref[...], gain_ref[...], sc_ref[0], sh_ref[0]).astype(BF16)
    rid = lax.broadcasted_iota(I32, (tm, 1), 0)

    def conv(part, j):
        u = _dot(hn_ref[...], wup_ref[part, j])
        prev = carry_ref[part, j]
        carry_ref[part, j] = u[tm - SUBLANES:, :]
        p1 = prev[SUBLANES - 1:SUBLANES, :]
        p2 = prev[SUBLANES - 2:SUBLANES - 1, :]
        s1 = jnp.where(rid == 0, p1, pltpu.roll(u, 1, 0))
        s2 = jnp.where(rid == 0, p2, jnp.where(rid == 1, p1, pltpu.roll(u, 2, 0)))
        cw = cw_ref[part, j]
        return cw[2:3] * u + cw[1:2] * s1 + cw[0:1] * s2 + cb_ref[part, j]

    for j in range(nf):
        ug = conv(0, j)
        uv = conv(1, j)
        act_ref[:, j * tf:(j + 1) * tf] = (ug * _sigmoid(ug) * uv).astype(BF16)
    out = res_ref[...] + g_ref[0] * _dot(act_ref[...], wdn_ref[...])
    if final_norm:
        out = _rms(out, gfin_ref[...])
    o_ref[0] = out


def _ffn(h, sh, sc, g, gain, w_up, conv_w, conv_b, w_down, gfin, final_norm, mixer=None, tm=512, tf=256):
    b, s, d = h.shape
    f = w_down.shape[0]
    nf = f // tf
    tm = min(tm, s)
    wup = w_up.astype(BF16).reshape(d, 2, nf, tf).transpose(1, 2, 0, 3)
    cw = conv_w.reshape(CONV_W, 2, nf, tf).transpose(1, 2, 0, 3)
    cb = conv_b.reshape(2, nf, 1, tf)
    wdn = w_down.astype(BF16)
    row = pl.BlockSpec((1, 1, d), lambda bi, i: (bi, 0, 0))
    tile = pl.BlockSpec((1, tm, d), lambda bi, i: (bi, i, 0))
    in_specs = [tile, row, row, row, _full((1, d)), _full(wup.shape, True), _full(cw.shape),
                _full(cb.shape), _full(wdn.shape, True), _full((1, d))]
    args = [h, sh, sc, g, gain.reshape(1, d), wup, cw, cb, wdn, gfin.reshape(1, d)]
    if mixer is not None:
        gm, y, yg, w_o = mixer
        in_specs += [row, pl.BlockSpec((1, d // LANES, tm, LANES), lambda bi, i: (bi, 0, i, 0)), tile,
                     _full((d, d), True)]
        args += [gm, y, yg, w_o.astype(BF16)]
    return pl.pallas_call(
        functools.partial(_ffn_body, final_norm, mixer is not None, nf),
        grid=(b, s // tm),
        in_specs=in_specs,
        out_specs=tile,
        out_shape=jax.ShapeDtypeStruct((b, s, d), F32),
        scratch_shapes=[pltpu.VMEM((2, nf, SUBLANES, tf), F32),
                        pltpu.VMEM((tm, d), BF16),
                        pltpu.VMEM((tm, f), BF16),
                        pltpu.VMEM((tm, d), F32)],
        name="conv_ffn",
        compiler_params=_params("arbitrary", "arbitrary"),
    )(*args)


def _fold_body(wuq_ref, wuk_ref, wuv_ref, wo_ref, wq_ref, wvo_ref):
    hi = lax.Precision.HIGHEST
    wq_ref[0] = lax.dot_general(wuq_ref[0], wuk_ref[0], NT, precision=hi,
                                preferred_element_type=F32) * (A_HEAD_DIM ** -0.5 * LOG2E)
    wvo_ref[0] = jnp.dot(wuv_ref[0], wo_ref[0], precision=hi, preferred_element_type=F32)


def _fold(w_uq, w_uk, w_uv, w_o):
    rq = w_uq.shape[0]
    h, c, dh = w_uk.shape
    d = w_o.shape[1]
    wuq = w_uq.reshape(rq, h, dh).transpose(1, 0, 2)
    wo = w_o.reshape(h, dh, d)
    per_head = lambda *shape: pl.BlockSpec((1,) + shape, lambda i: (i, 0, 0))
    wq, wvo = pl.pallas_call(
        _fold_body,
        grid=(h,),
        in_specs=[per_head(rq, dh), per_head(c, dh), per_head(c, dh), per_head(dh, d)],
        out_specs=[per_head(rq, c), per_head(c, d)],
        out_shape=[jax.ShapeDtypeStruct((h, rq, c), F32), jax.ShapeDtypeStruct((h, c, d), F32)],
        name="dsa_fold",
        compiler_params=_params("arbitrary"),
    )(wuq, w_uk, w_uv, wo)
    return (wq.transpose(1, 0, 2).reshape(rq, h * c).astype(BF16),
            wvo.reshape(h * c, d).astype(BF16))


def _dsa_proj_body(x_ref, sh_ref, sc_ref, gain_ref, win_ref, qn_ref, kvn_ref, kg_ref, kb_ref,
                   wq_ref, wqi_ref, qabs_ref, qidx_ref, aux_ref, kpad_ref, caug_ref):
    tm = x_ref.shape[1]
    hn = _rms_mod(x_ref[0], gain_ref[...], sc_ref[0], sh_ref[0]).astype(BF16)
    proj = _dot(hn, win_ref[...])
    ql = _rms(proj[:, :A_Q_RANK], qn_ref[...]).astype(BF16)
    ck = _rms(proj[:, A_Q_RANK:A_Q_RANK + A_KV_RANK], kvn_ref[...])
    slab = proj[:, A_Q_RANK + A_KV_RANK:]
    lane = lax.broadcasted_iota(I32, (1, LANES), 1)
    is_k = lane < IDX_DIM
    mu = jnp.sum(jnp.where(is_k, slab, 0.0), axis=-1, keepdims=True) * (1.0 / IDX_DIM)
    dk = jnp.where(is_k, slab - mu, 0.0)
    var = jnp.sum(dk * dk, axis=-1, keepdims=True) * (1.0 / IDX_DIM)
    kpad_ref[0] = (dk * lax.rsqrt(var + EPS) * kg_ref[...] + kb_ref[...]).astype(BF16)
    aux_ref[0] = slab * (IDX_HEADS ** -0.5 * IDX_DIM ** -0.5)
    qabs_ref[0] = _dot(ql, wq_ref[...]).astype(BF16)
    qidx_ref[0] = _dot(ql, wqi_ref[...]).astype(BF16)
    pos = pl.program_id(1) * tm + lax.broadcasted_iota(I32, (tm, 1), 0)
    p_hi = (pos >> 6).astype(F32)
    p_lo = (pos & 63).astype(F32)
    extra = jnp.where(lane < 2, p_hi, jnp.where(lane < 4, p_lo, 0.0))
    caug_ref[0, :, :A_KV_RANK] = ck.astype(BF16)
    caug_ref[0, :, A_KV_RANK:] = extra.astype(BF16)


def _dsa_proj(x, sh, sc, gain, w_in, q_norm, kv_norm, kidx_g, kidx_b, wq, w_qi, tm=512):
    b, s, d = x.shape
    tm = min(tm, s)
    n_in = w_in.shape[1]
    win = jnp.pad(w_in, ((0, 0), (0, 4 * LANES - n_in))).astype(BF16)
    wqi = jnp.pad(w_qi.reshape(A_Q_RANK, IDX_HEADS, IDX_DIM),
                  ((0, 0), (0, 0), (0, LANES - IDX_DIM))).reshape(A_Q_RANK, IDX_HEADS * LANES).astype(BF16)
    kg = jnp.pad(kidx_g, (0, LANES - IDX_DIM)).reshape(1, LANES)
    kb = jnp.pad(kidx_b, (0, LANES - IDX_DIM)).reshape(1, LANES)
    row = pl.BlockSpec((1, 1, d), lambda bi, i: (bi, 0, 0))
    tile = lambda n: pl.BlockSpec((1, tm, n), lambda bi, i: (bi, i, 0))
    nq, ni = A_HEADS * A_KV_RANK, IDX_HEADS * LANES
    return pl.pallas_call(
        _dsa_proj_body,
        grid=(b, s // tm),
        in_specs=[tile(d), row, row, _full((1, d)), _full(win.shape), _full((1, A_Q_RANK)),
                  _full((1, A_KV_RANK)), _full((1, LANES)), _full((1, LANES)),
                  _full(wq.shape), _full(wqi.shape)],
        out_specs=[tile(nq), tile(ni), tile(LANES), tile(LANES), tile(2 * A_KV_RANK)],
        out_shape=[jax.ShapeDtypeStruct((b, s, nq), BF16), jax.ShapeDtypeStruct((b, s, ni), BF16),
                   jax.ShapeDtypeStruct((b, s, LANES), F32), jax.ShapeDtypeStruct((b, s, LANES), BF16),
                   jax.ShapeDtypeStruct((b, s, 2 * A_KV_RANK), BF16)],
        name="dsa_proj",
        compiler_params=_params("arbitrary", "arbitrary"),
    )(x, sh, sc, gain.reshape(1, d), win, q_norm.reshape(1, -1), kv_norm.reshape(1, -1), kg, kb, wq, wqi)


def _bf16_digits(v):
    hi = float(np.asarray(v, np.float32).astype(BF16).astype(np.float32))
    lo = float(np.asarray(v - hi, np.float32).astype(BF16).astype(np.float32))
    return hi, lo


def _dsa_attn_body(x_ref, gm_ref, qabs_ref, qidx_ref, aux_ref, kpad_ref, caug_ref, wvo_ref, o_ref,
                   keys_ref, keyst_ref, qaug_ref, qis_ref, wb_ref, bias_ref, ahead_ref, t_ref, m_ref, al_ref,
                   l_ref, acc_ref, s_ref, p_ref):
    tq = x_ref.shape[1]
    seq = kpad_ref.shape[1]
    c = A_KV_RANK
    nj = KB // LANES
    q0 = pl.program_id(1) * tq
    kend = q0 + tq
    n_kb = (kend + KB - 1) // KB
    tail = (n_kb - 1) * KB
    lane = lax.broadcasted_iota(I32, (1, LANES), 1)
    trow = lax.broadcasted_iota(I32, (tq, 1), 0)
    slopes = [LOG2E * 2.0 ** (-8.0 * (h + 1) / A_HEADS) for h in range(A_HEADS)]

    for h in range(A_HEADS):
        rows = slice(h * tq, (h + 1) * tq)
        hi, lo = _bf16_digits(slopes[h])
        digits = jnp.where(lane == 0, 64.0 * hi, jnp.where(lane == 1, 64.0 * lo,
                           jnp.where(lane == 2, hi, jnp.where(lane == 3, lo, 0.0))))
        qaug_ref[rows, :c] = qabs_ref[0, :, h * c:(h + 1) * c]
        qaug_ref[rows, c:] = jnp.broadcast_to(digits, (tq, LANES)).astype(BF16)
    aux = aux_ref[0]
    for h in range(IDX_HEADS):
        qis_ref[h * tq:(h + 1) * tq, :] = qidx_ref[0, :, h * LANES:(h + 1) * LANES]
        wb_ref[h] = jnp.broadcast_to(aux[:, IDX_DIM + h:IDX_DIM + h + 1], (tq, LANES))

    qchunk = (q0 + trow) >> 6

    def to_key(sc):
        bits = pltpu.bitcast(sc, I32)
        return bits ^ ((bits >> 31) & 0x7FFFFFFF)

    def score_block(kb, carry):
        off = pl.multiple_of(kb * KB, KB)
        z = _dot_nt(qis_ref[...], kpad_ref[0, pl.ds(off, KB), :])
        for j in range(nj):
            sc = jnp.zeros((tq, LANES), F32)
            for h in range(IDX_HEADS):
                sc = sc + wb_ref[h] * jnp.maximum(z[h * tq:(h + 1) * tq, j * LANES:(j + 1) * LANES], 0.0)
            kchunk = (off + j * LANES + lane) >> 6
            sc = jnp.where(kchunk <= qchunk, sc, -jnp.inf)
            keys_ref[kb * nj + j] = to_key(sc)
            keyst_ref[pl.ds(pl.multiple_of(off + j * LANES, LANES), LANES), :] = to_key(sc.T)
        return carry

    lax.fori_loop(0, n_kb, score_block, 0)

    slab = 8 * SUBLANES
    krow = lax.broadcasted_iota(I32, (slab, LANES), 0)

    def count(pred):
        def body(kb, acc):
            for u in range(KB // slab):
                r0 = pl.multiple_of(kb * KB + u * slab, slab)
                acc = jnp.where(pred(keyst_ref[pl.ds(r0, slab), :], r0 + krow), acc + 1.0, acc)
            return acc
        part = lax.fori_loop(0, n_kb, body, jnp.zeros((slab, LANES), F32))
        return jnp.sum(part, axis=0, keepdims=True)

    def per_row(v):
        return pltpu.bitcast(pltpu.bitcast(jnp.broadcast_to(v, (LANES, LANES)), F32).T, I32)

    t_ref[...] = jnp.full(t_ref.shape, INT_MIN, I32)

    @pl.when(kend > TOPK_MAX)
    def _():
        kf = float(TOPK_MAX)
        n_pos = count(lambda k, s: k >= 0)
        t0 = jnp.where(n_pos >= kf, 0, INT_MIN).astype(I32)

        def bit_step(i, t):
            cand = t | jnp.left_shift(jnp.int32(1), 30 - i)
            return jnp.where(count(lambda k, s: k >= cand) >= kf, cand, t)

        t = lax.fori_loop(0, 31, bit_step, t0)
        t_ref[...] = per_row(t)
        n_ge = count(lambda k, s: k >= t)

        @pl.when(jnp.max(n_ge) > kf)
        def _():
            need = kf - count(lambda k, s: k > t)
            n_bits = max(1, int(seq - 1).bit_length())

            def idx_step(i, x):
                cand = x | jnp.left_shift(jnp.int32(1), n_bits - 1 - i)
                below = count(lambda k, s: (k == t) & (s < cand))
                return jnp.where(below < need, cand, x)

            last = per_row(lax.fori_loop(0, n_bits, idx_step, jnp.zeros((1, LANES), I32)))
            thr_ = t_ref[...]
            demoted = jnp.where(thr_ == INT_MIN, thr_, thr_ - 1)

            def demote(j, carry):
                k = keys_ref[j]
                keys_ref[j] = jnp.where((k == thr_) & (j * LANES + lane > last), demoted, k)
                return carry

            lax.fori_loop(0, n_kb * nj, demote, 0)

    m_ref[...] = jnp.full(m_ref.shape, NEG, F32)
    l_ref[...] = jnp.zeros_like(l_ref)
    acc_ref[...] = jnp.zeros_like(acc_ref)
    thr = t_ref[...]
    hg = A_HEADS // ATT_GROUPS
    group = lambda g: slice(g * hg * tq, (g + 1) * hg * tq)

    def attend(kb, slot, is_tail):
        cb = caug_ref[0, pl.ds(pl.multiple_of(kb * KB, KB), KB), :]
        for g in range(ATT_GROUPS):
            s_ref[slot, group(g), :] = _dot_nt(qaug_ref[group(g), :], cb)
        for j in range(nj):
            cols = slice(j * LANES, (j + 1) * LANES)
            picked = keys_ref[kb * nj + j] >= thr
            if is_tail:
                kcol = tail + j * LANES + lane
                picked = picked & ((kcol >> 6) <= ((q0 + trow) >> 6))
                ahead_ref[:, cols] = -2.0 * jnp.maximum(kcol - (q0 + trow), 0).astype(F32)
            bias_ref[slot, :, cols] = jnp.where(picked, 0.0, NEG)
        for g in range(ATT_GROUPS):
            for h in range(g * hg, (g + 1) * hg):
                rows = slice(h * tq, (h + 1) * tq)
                tiles = []
                for j in range(nj):
                    cols = slice(j * LANES, (j + 1) * LANES)
                    t_ = s_ref[slot, rows, cols] + bias_ref[slot, :, cols]
                    tiles.append(t_ + slopes[h] * ahead_ref[:, cols] if is_tail else t_)
                top = functools.reduce(jnp.maximum, tiles)
                m_old = m_ref[rows, :]
                m_new = jnp.maximum(m_old, jnp.broadcast_to(jnp.max(top, axis=-1, keepdims=True), top.shape))
                ps = [jnp.exp2(t_ - m_new) for t_ in tiles]
                p_ref[slot, rows, :] = jnp.concatenate(ps, axis=1).astype(BF16)
                alpha = jnp.exp2(m_old - m_new)
                l_ref[rows, :] = l_ref[rows, :] * alpha + functools.reduce(jnp.add, ps)
                al_ref[rows, :] = alpha
                m_ref[rows, :] = m_new
            acc_ref[group(g), :] = (acc_ref[group(g), :] * al_ref[group(g), :]
                                    + _dot(p_ref[slot, group(g), :], cb[:, :c]))

    n_past = n_kb - 1

    def past_pair(i, carry):
        attend(2 * i, 0, False)
        attend(2 * i + 1, 1, False)
        return carry

    lax.fori_loop(0, n_past // 2, past_pair, 0)

    @pl.when(n_past % 2 == 1)
    def _():
        attend(n_past - 1, 0, False)

    attend(n_past, 1, True)

    y = jnp.zeros((tq, x_ref.shape[2]), F32)
    for pair in range(A_HEADS // 2):
        parts = []
        for h in (2 * pair, 2 * pair + 1):
            rows = slice(h * tq, (h + 1) * tq)
            parts.append((acc_ref[rows, :] / jnp.sum(l_ref[rows, :], axis=-1, keepdims=True)).astype(BF16))
        y = y + _dot(jnp.concatenate(parts, axis=1), wvo_ref[2 * pair * c:(2 * pair + 2) * c, :])
    o_ref[0] = x_ref[0] + gm_ref[0] * y


def _dsa_attn(x, gm, qabs, qidx, aux, kpad, caug, wvo):
    b, s, d = x.shape
    tq = TQ
    assert tq == LANES and s % KB == 0
    hq = A_HEADS * tq
    row = pl.BlockSpec((1, 1, d), lambda bi, i: (bi, 0, 0))
    tile = lambda n: pl.BlockSpec((1, tq, n), lambda bi, i: (bi, i, 0))
    per_batch = lambda n: pl.BlockSpec((1, s, n), lambda bi, i: (bi, 0, 0))
    return pl.pallas_call(
        _dsa_attn_body,
        grid=(b, s // tq),
        in_specs=[tile(d), row, tile(qabs.shape[2]), tile(qidx.shape[2]), tile(LANES),
                  per_batch(LANES), per_batch(2 * A_KV_RANK), _full(wvo.shape)],
        out_specs=tile(d),
        out_shape=jax.ShapeDtypeStruct((b, s, d), F32),
        scratch_shapes=[pltpu.VMEM((s // LANES, tq, LANES), I32),
                        pltpu.VMEM((s, tq), I32),
                        pltpu.VMEM((hq, 2 * A_KV_RANK), BF16),
                        pltpu.VMEM((IDX_HEADS * tq, LANES), BF16),
                        pltpu.VMEM((IDX_HEADS, tq, LANES), F32),
                        pltpu.VMEM((2, tq, KB), F32),
                        pltpu.VMEM((tq, KB), F32),
                        pltpu.VMEM((tq, LANES), I32),
                        pltpu.VMEM((hq, LANES), F32),
                        pltpu.VMEM((hq, LANES), F32),
                        pltpu.VMEM((hq, LANES), F32),
                        pltpu.VMEM((hq, A_KV_RANK), F32),
                        pltpu.VMEM((2, hq, KB), F32),
                        pltpu.VMEM((2, hq, KB), BF16)],
        name="dsa_attn",
        compiler_params=_params("arbitrary", "arbitrary"),
    )(x, gm, qabs, qidx, aux, kpad, caug, wvo)


def _rwkv_proj_body(h_ref, sh_ref, sc_ref, gain_ref, mu_ref, wr_ref, wk_ref, wv_ref, w1_ref, w2_ref,
                    a1_ref, a2_ref, g1_ref, g2_ref, w0_ref, a0_ref,
                    r_ref, k_ref, v_ref, lw_ref, a_ref, g_ref, carry_ref):
    tm = h_ref.shape[1]

    @pl.when(pl.program_id(1) == 0)
    def _():
        carry_ref[...] = jnp.zeros_like(carry_ref)

    hn = _rms_mod(h_ref[0], gain_ref[...], sc_ref[0], sh_ref[0])
    prev = carry_ref[SUBLANES - 1:SUBLANES, :]
    carry_ref[...] = hn[tm - SUBLANES:, :]
    rid = lax.broadcasted_iota(I32, (tm, 1), 0)
    dx = jnp.where(rid == 0, prev, pltpu.roll(hn, 1, 0)) - hn
    mix = lambda j: (hn + dx * mu_ref[j:j + 1, :]).astype(BF16)

    def put(ref, val):
        for p in range(val.shape[1] // LANES):
            ref[0, p] = val[:, p * LANES:(p + 1) * LANES].astype(ref.dtype)

    put(r_ref, _dot(mix(0), wr_ref[...]))
    put(k_ref, _dot(mix(2), wk_ref[...]))
    put(v_ref, _dot(mix(3), wv_ref[...]))
    ww = w0_ref[...] + _dot(jnp.tanh(_dot(mix(1), w1_ref[...])).astype(BF16), w2_ref[...])
    softplus = jnp.maximum(-ww, 0.0) + jnp.log(1.0 + jnp.exp(-jnp.abs(ww)))
    put(lw_ref, -jnp.exp(-softplus - 0.5))
    put(a_ref, _sigmoid(a0_ref[...] + _dot(_dot(mix(4), a1_ref[...]).astype(BF16), a2_ref[...])))
    g_ref[0] = _dot(_sigmoid(_dot(mix(5), g1_ref[...])).astype(BF16), g2_ref[...]).astype(g_ref.dtype)


def _pad_rank(w_down, w_up):
    r = w_down.shape[1]
    p = -r % LANES
    return (jnp.pad(w_down, ((0, 0), (0, p))).astype(BF16), jnp.pad(w_up, ((0, p), (0, 0))).astype(BF16))


def _rwkv_proj(h, sh, sc, gain, mu, w_r, w_k, w_v, w0, w1, w2, a0, a1, a2, g1, g2, tm=512):
    b, s, d = h.shape
    tm = min(tm, s)
    w1p, w2p = _pad_rank(w1, w2)
    a1p, a2p = _pad_rank(a1, a2)
    g1p, g2p = _pad_rank(g1, g2)
    row = pl.BlockSpec((1, 1, d), lambda bi, i: (bi, 0, 0))
    tile = pl.BlockSpec((1, tm, d), lambda bi, i: (bi, i, 0))
    big = [w.astype(BF16) for w in (w_r, w_k, w_v)]
    small = [w1p, w2p, a1p, a2p, g1p, g2p]
    np_ = d // LANES
    ptile = pl.BlockSpec((1, np_, tm, LANES), lambda bi, i: (bi, 0, i, 0))
    pout = lambda dt: jax.ShapeDtypeStruct((b, np_, s, LANES), dt)
    return pl.pallas_call(
        _rwkv_proj_body,
        grid=(b, s // tm),
        in_specs=[tile, row, row, _full((1, d)), _full(mu.shape)] + [_full(w.shape) for w in big + small]
                 + [_full((1, d)), _full((1, d))],
        out_specs=[ptile] * 5 + [tile],
        out_shape=[pout(BF16), pout(BF16), pout(BF16), pout(F32), pout(BF16),
                   jax.ShapeDtypeStruct((b, s, d), BF16)],
        scratch_shapes=[pltpu.VMEM((SUBLANES, d), F32)],
        name="rwkv_proj",
        compiler_params=_params("arbitrary", "arbitrary"),
    )(h, sh, sc, gain.reshape(1, d), mu, *big, *small, w0.reshape(1, d), a0.reshape(1, d))


def _rwkv_scan_body(n_chunks, r_ref, k_ref, v_ref, lw_ref, a_ref, kk_ref, ka_ref, rk_ref, gg_ref, gb_ref,
                    y_ref, st_ref, w_s, rt_s, u0_s, y0_s, bon_s, arb_s, bkt_s, plt_s):
    L = SCAN_L
    P2 = 2 * L
    n_pairs = r_ref.shape[1]

    @pl.when(pl.program_id(1) == 0)
    def _():
        st_ref[...] = jnp.zeros_like(st_ref)

    lane = lax.broadcasted_iota(I32, (1, LANES), 1)
    head0 = lane < B_HEAD
    m0 = jnp.where(head0, 1.0, 0.0)
    m1 = 1.0 - m0
    ri = lax.broadcasted_iota(I32, (P2, P2), 0)
    ci = lax.broadcasted_iota(I32, (P2, P2), 1)
    same = (ri >> 6) == (ci >> 6)
    strict = same & ((ci & 63) < (ri & 63))
    incl = same & ((ci & 63) <= (ri & 63))
    eye = jnp.where(ri == ci, 1.0, 0.0)
    tril = jnp.where(lax.broadcasted_iota(I32, (L, L), 1) <= lax.broadcasted_iota(I32, (L, L), 0),
                     1.0, 0.0).astype(BF16)
    def seg_sum(x):
        s0 = jnp.sum(x * m0, axis=-1, keepdims=True)
        s1 = jnp.sum(x * m1, axis=-1, keepdims=True)
        return jnp.where(head0, s0, s1)

    stack = lambda x: jnp.concatenate([x * m0, x * m1], axis=0)
    fold = lambda x: x[:L] + x[L:]

    bdot = lambda a, b: _dot(a.astype(BF16), b.astype(BF16))

    each = lambda f, *xs: [f(*a) for a in zip(*xs)]

    def prepare_pair(p, carry):
        k_k, k_a, r_k = kk_ref[p], ka_ref[p], rk_ref[p]
        cs = range(n_chunks)
        r, kraw, v, lw, asig = ([ref[0, p, c * L:(c + 1) * L, :].astype(F32) for c in cs]
                                for ref in (r_ref, k_ref, v_ref, lw_ref, a_ref))
        kk = each(lambda x: x * k_k, kraw)
        kkn = each(lambda x: x / jnp.maximum(jnp.sqrt(seg_sum(x * x)), 1e-12), kk)
        kmod = each(lambda x, a: x * (1.0 + (a - 1.0) * k_a), kraw, asig)
        bvec = each(jnp.multiply, kkn, asig)
        gc = each(lambda x: _dot(tril, jnp.concatenate(_split(x), axis=1)), lw)
        g = each(lambda x: x[:, :LANES] + x[:, LANES:], gc)
        g_last = each(lambda x: x[L - 1:L, :], g)
        at = each(lambda kn, x, w: -kn * jnp.exp(x - w), kkn, g, lw)
        rt = each(lambda x, y: x * jnp.exp(y), r, g)
        inv = each(lambda x: jnp.exp(-x), g)
        to_end = each(lambda x, y: jnp.exp(x - y), g_last, g)
        bk_end = each(lambda b_, k_, e: jnp.concatenate([b_ * e, k_ * e], axis=0), bvec, kmod, to_end)
        lhs = each(lambda a, b_: jnp.concatenate([stack(a), stack(b_)], axis=0).astype(BF16), at, rt)
        rhs = each(lambda b_, k_, i: jnp.concatenate([b_ * i, b_ * i, k_ * i, k_ * i], axis=0).astype(BF16),
                   bvec, kmod, inv)
        mm = each(_dot_nt, lhs, rhs)
        n_ab = each(lambda x: jnp.where(strict, x[:P2, :P2], 0.0), mm)
        a_kk = each(lambda x: jnp.concatenate([jnp.where(strict, x[:P2, P2:], 0.0),
                                               jnp.where(incl, x[P2:, P2:], 0.0)], axis=0), mm)
        a_rb = each(lambda x: jnp.where(incl, x[P2:, :P2], 0.0), mm)

        tmat = each(lambda x: eye + x, n_ab)
        pw = each(bdot, n_ab, n_ab)
        levels = int(L).bit_length() - 2
        for lvl in range(levels):
            if lvl < levels - 1:
                both = each(lambda q, t: bdot(q, jnp.concatenate([q, t], axis=1)), pw, tmat)
                tmat = each(lambda t, b_: t + b_[:, P2:], tmat, both)
                pw = each(lambda b_: b_[:, :P2], both)
            else:
                tmat = each(lambda t, q: t + bdot(q, t), tmat, pw)

        akv = each(lambda a, x: bdot(a, stack(x)), a_kk, v)
        tw = each(lambda t, a, x: bdot(t, jnp.concatenate([stack(a), stack(fold(x[:P2]))], axis=1)),
                  tmat, at, akv)
        for c in cs:
            w_s[p, c] = fold(tw[c][:, :LANES]).astype(BF16)
            rt_s[p, c] = rt[c].astype(BF16)
            u0_s[p, c] = fold(tw[c][:, LANES:])
            y0_s[p, c] = fold(akv[c][P2:])
            bon_s[p, c] = seg_sum(r[c] * kmod[c] * r_k) * v[c]
            arb_s[p, c] = a_rb[c].astype(BF16)
            bkt_s[p, c] = bk_end[c].T.astype(BF16)
            plt_s[p, c] = jnp.broadcast_to(jnp.exp(g_last[c]), (P2, LANES)).T
        return carry

    lax.fori_loop(0, n_pairs, prepare_pair, 0)

    def advance(c, carry):
        rows = pl.ds(pl.multiple_of(c * L, L), L)
        ps = range(n_pairs)
        st = [st_ref[p] for p in ps]
        ws = [_dot(jnp.concatenate([w_s[p, c], rt_s[p, c]], axis=0), st[p].astype(BF16)) for p in ps]
        u = [ws[p][:L] + u0_s[p, c] for p in ps]
        uv = [jnp.concatenate([u[p].astype(BF16), v_ref[0, p, rows, :]], axis=0) for p in ps]
        upd = [_dot(bkt_s[p, c], uv[p]) for p in ps]
        for p in ps:
            st_ref[p] = st[p] * plt_s[p, c] + jnp.where(same, upd[p], 0.0)
        au = [_dot(arb_s[p, c], stack(u[p]).astype(BF16)) for p in ps]
        for p in ps:
            y = ws[p][L:] + fold(au[p]) + y0_s[p, c]
            mu = seg_sum(y) * (1.0 / B_HEAD)
            dy = y - mu
            var = seg_sum(dy * dy) * (1.0 / B_HEAD)
            y_ref[0, p, rows, :] = (dy * lax.rsqrt(var + GN_EPS) * gg_ref[p] + gb_ref[p]
                                    + bon_s[p, c]).astype(y_ref.dtype)
        return carry

    lax.fori_loop(0, n_chunks, advance, 0)


def _rwkv_scan(r, k, v, lw, a, k_k, k_a, r_k, gn_g, gn_b, lb=512):
    b, n_pairs, s, _ = r.shape
    lb = min(lb, s)
    nc = lb // SCAN_L
    tile = pl.BlockSpec((1, n_pairs, lb, LANES), lambda bi, i: (bi, 0, i, 0))
    vec = lambda x: x.reshape(n_pairs, 1, LANES)
    per_chunk = lambda rows, dt: pltpu.VMEM((n_pairs, nc, rows, LANES), dt)
    return pl.pallas_call(
        functools.partial(_rwkv_scan_body, nc),
        grid=(b, s // lb),
        in_specs=[tile] * 5 + [_full((n_pairs, 1, LANES))] * 5,
        out_specs=tile,
        out_shape=jax.ShapeDtypeStruct(r.shape, BF16),
        scratch_shapes=[pltpu.VMEM((n_pairs, LANES, LANES), F32),
                        per_chunk(SCAN_L, BF16), per_chunk(SCAN_L, BF16),
                        per_chunk(SCAN_L, F32), per_chunk(SCAN_L, F32), per_chunk(SCAN_L, F32),
                        per_chunk(2 * SCAN_L, BF16), per_chunk(2 * SCAN_L, BF16),
                        per_chunk(2 * SCAN_L, F32)],
        name="rwkv_scan",
        compiler_params=_params("arbitrary", "arbitrary"),
    )(r, k, v, lw, a, vec(k_k), vec(k_a), vec(r_k), vec(gn_g), vec(gn_b))


def kernel(x, c, ada_w, ada_b, norm_mix, norm_ffn, norm_final, a_w_in, a_q_norm, a_kv_norm, a_kidx_g, a_kidx_b, a_w_uq, a_w_qi, a_w_uk, a_w_uv, a_w_o, b_mu, b_w_r, b_w_k, b_w_v, b_w_o, b_w0, b_w1, b_w2, b_a0, b_a1, b_a2, b_g1, b_g2, b_k_k, b_k_a, b_r_k, b_gn_g, b_gn_b, f_w_up, f_conv_w, f_conv_b, f_w_down):
    b, s, d = x.shape
    mod = _ada(c, ada_w, ada_b).reshape(ada_w.shape[0], b, 6, 1, d)
    sh_m, sc_m, g_m, sh_f, sc_f, g_f = (mod[:, :, j] for j in range(6))

    wq, wvo = _fold(a_w_uq[0], a_w_uk[0], a_w_uv[0], a_w_o[0])
    qabs, qidx, aux, kpad, caug = _dsa_proj(x, sh_m[0], sc_m[0], norm_mix[0], a_w_in[0], a_q_norm[0],
                                            a_kv_norm[0], a_kidx_g[0], a_kidx_b[0], wq, a_w_qi[0])
    h = _dsa_attn(x, g_m[0], qabs, qidx, aux, kpad, caug, wvo)
    h = _ffn(h, sh_f[0], sc_f[0], g_f[0], norm_ffn[0], f_w_up[0], f_conv_w[0], f_conv_b[0], f_w_down[0],
             norm_final, False)

    r, k, v, lw, a, g = _rwkv_proj(h, sh_m[1], sc_m[1], norm_mix[1], b_mu[0], b_w_r[0], b_w_k[0], b_w_v[0],
                                   b_w0[0], b_w1[0], b_w2[0], b_a0[0], b_a1[0], b_a2[0], b_g1[0], b_g2[0])
    y = _rwkv_scan(r, k, v, lw, a, b_k_k[0], b_k_a[0], b_r_k[0].reshape(-1), b_gn_g[0], b_gn_b[0])
    return _ffn(h, sh_f[1], sc_f[1], g_f[1], norm_ffn[1], f_w_up[1], f_conv_w[1], f_conv_b[1], f_w_down[1],
                norm_final, True, mixer=(g_m[1], y, g, b_w_o[0]))
```

```python
import functools

import numpy as np
import jax
import jax.numpy as jnp
from jax import lax
from jax.experimental import pallas as pl
from jax.experimental.pallas import tpu as pltpu

F32 = jnp.float32
BF16 = jnp.bfloat16
I32 = jnp.int32

EPS = 1e-6
CHUNK = 64
A_HEADS = 16
A_HEAD_DIM = 64
A_Q_RANK = 256
A_KV_RANK = 128
IDX_HEADS = 8
IDX_DIM = 64
TOPK_MAX = 256
B_HEAD = 64
GN_EPS = 64e-5
CONV_W = 3

LANES = 128
SUBLANES = 8
VMEM_LIMIT = 56 * 1024 * 1024
NEG = -1e30
INT_MIN = -(2 ** 31)

LOG2E = 1.4426950408889634
TQ = 128
KB = 512
ATT_GROUPS = 2
SCAN_L = 64
NT = (((1,), (1,)), ((), ()))


def _dot(a, b):
    return jnp.dot(a, b, preferred_element_type=F32)


def _dot_nt(a, b):
    return lax.dot_general(a, b, NT, preferred_element_type=F32)


def _split(x):
    hi = x.astype(BF16)
    lo = (x - hi.astype(F32)).astype(BF16)
    return hi, lo


def _sigmoid(x):
    return 1.0 / (1.0 + jnp.exp(-x))


def _rms(x, gain):
    return x * lax.rsqrt(jnp.mean(x * x, axis=-1, keepdims=True) + EPS) * gain


def _rms_mod(x, gain, sc, sh):
    return _rms(x, gain) * (1.0 + sc) + sh


def _params(*sem):
    return pltpu.CompilerParams(dimension_semantics=sem, vmem_limit_bytes=VMEM_LIMIT)


def _full(shape, single=False):
    n = len(shape)
    mode = dict(pipeline_mode=pl.Buffered(1)) if single else {}
    return pl.BlockSpec(shape, lambda *_: (0,) * n, **mode)


def _ada_body(c_ref, w_ref, b_ref, o_ref):
    c = c_ref[...]
    o_ref[0] = _dot(c * _sigmoid(c), w_ref[0]) + b_ref[0]


def _ada(c, ada_w, ada_b):
    depth, d, n = ada_w.shape
    b = c.shape[0]
    tn = n // 4
    return pl.pallas_call(
        _ada_body,
        grid=(depth, n // tn),
        in_specs=[pl.BlockSpec((b, d), lambda i, j: (0, 0)),
                  pl.BlockSpec((1, d, tn), lambda i, j: (i, 0, j)),
                  pl.BlockSpec((1, 1, tn), lambda i, j: (i, 0, j))],
        out_specs=pl.BlockSpec((1, b, tn), lambda i, j: (i, 0, j)),
        out_shape=jax.ShapeDtypeStruct((depth, b, n), F32),
        name="ada_mod",
        compiler_params=_params("arbitrary", "arbitrary"),
    )(c, ada_w, ada_b.reshape(depth, 1, n))


def _ffn_body(final_norm, mixer_out, nf, h_ref, sh_ref, sc_ref, g_ref, gain_ref, wup_ref, cw_ref, cb_ref,
              wdn_ref, gfin_ref, *rest):
    if mixer_out:
        gm_ref, y_ref, yg_ref, wo_ref, o_ref, carry_ref, hn_ref, act_ref, res_ref = rest
    else:
        o_ref, carry_ref, hn_ref, act_ref, res_ref = rest
    tm = h_ref.shape[1]
    f = wdn_ref.shape[0]
    tf = f // nf

    @pl.when(pl.program_id(1) == 0)
    def _():
        carry_ref[...] = jnp.zeros_like(carry_ref)

    if mixer_out:
        y = jnp.concatenate([y_ref[0, p] for p in range(y_ref.shape[1])], axis=1)
        res_ref[...] = h_ref[0] + gm_ref[0] * _dot(y * yg_ref[0], wo_ref[...])
    else:
        res_ref[...] = h_ref[0]
    hn_ref[...] = _rms_mod(res_ref[...], gain_ref[...], sc_ref[0], sh_ref[0]).astype(BF16)
    rid = lax.broadcasted_iota(I32, (tm, 1), 0)

    def conv(part, j):
        cols = slice(part * f + j * tf, part * f + (j + 1) * tf)
        u = _dot(hn_ref[...], wup_ref[:, cols])
        prev = carry_ref[part, j]
        carry_ref[part, j] = u[tm - SUBLANES:, :]
        p1 = prev[SUBLANES - 1:SUBLANES, :]
        p2 = prev[SUBLANES - 2:SUBLANES - 1, :]
        s1 = jnp.where(rid == 0, p1, pltpu.roll(u, 1, 0))
        s2 = jnp.where(rid == 0, p2, jnp.where(rid == 1, p1, pltpu.roll(u, 2, 0)))
        cw = cw_ref[:, cols]
        return cw[2:3] * u + cw[1:2] * s1 + cw[0:1] * s2 + cb_ref[:, cols]

    for j in range(nf):
        ug = conv(0, j)
        uv = conv(1, j)
        act_ref[:, j * tf:(j + 1) * tf] = (ug * _sigmoid(ug) * uv).astype(BF16)
    out = res_ref[...] + g_ref[0] * _dot(act_ref[...], wdn_ref[...])
    if final_norm:
        out = _rms(out, gfin_ref[...])
    o_ref[0] = out


def _ffn(h, sh, sc, g, gain, w_up, conv_w, conv_b, w_down, gfin, final_norm, mixer=None, tm=512, tf=256):
    b, s, d = h.shape
    f = w_down.shape[0]
    nf = f // tf
    tm = min(tm, s)
    wup = w_up.astype(BF16)
    cw = conv_w
    cb = conv_b.reshape(1, 2 * f)
    wdn = w_down.astype(BF16)
    row = pl.BlockSpec((1, 1, d), lambda bi, i: (bi, 0, 0))
    tile = pl.BlockSpec((1, tm, d), lambda bi, i: (bi, i, 0))
    in_specs = [tile, row, row, row, _full((1, d)), _full(wup.shape, True), _full(cw.shape),
                _full(cb.shape), _full(wdn.shape, True), _full((1, d))]
    args = [h, sh, sc, g, gain.reshape(1, d), wup, cw, cb, wdn, gfin.reshape(1, d)]
    if mixer is not None:
        gm, y, yg, w_o = mixer
        in_specs += [row, pl.BlockSpec((1, d // LANES, tm, LANES), lambda bi, i: (bi, 0, i, 0)), tile,
                     _full((d, d), True)]
        args += [gm, y, yg, w_o.astype(BF16)]
    return pl.pallas_call(
        functools.partial(_ffn_body, final_norm, mixer is not None, nf),
        grid=(b, s // tm),
        in_specs=in_specs,
        out_specs=tile,
        out_shape=jax.ShapeDtypeStruct((b, s, d), F32),
        scratch_shapes=[pltpu.VMEM((2, nf, SUBLANES, tf), F32),
                        pltpu.VMEM((tm, d), BF16),
                        pltpu.VMEM((tm, f), BF16),
                        pltpu.VMEM((tm, d), F32)],
        name="conv_ffn",
        compiler_params=_params("arbitrary", "arbitrary"),
    )(*args)


def _fold_body(wuq_ref, wuk_ref, wuv_ref, wo_ref, wq_ref, wvo_ref):
    hi = lax.Precision.HIGHEST
    wq_ref[0] = lax.dot_general(wuq_ref[0], wuk_ref[0], NT, precision=hi,
                                preferred_element_type=F32) * (A_HEAD_DIM ** -0.5 * LOG2E)
    wvo_ref[0] = jnp.dot(wuv_ref[0], wo_ref[0], precision=hi, preferred_element_type=F32)


def _fold(w_uq, w_uk, w_uv, w_o):
    rq = w_uq.shape[0]
    h, c, dh = w_uk.shape
    d = w_o.shape[1]
    wuq = w_uq.reshape(rq, h, dh).transpose(1, 0, 2)
    wo = w_o.reshape(h, dh, d)
    per_head = lambda *shape: pl.BlockSpec((1,) + shape, lambda i: (i, 0, 0))
    wq, wvo = pl.pallas_call(
        _fold_body,
        grid=(h,),
        in_specs=[per_head(rq, dh), per_head(c, dh), per_head(c, dh), per_head(dh, d)],
        out_specs=[per_head(rq, c), per_head(c, d)],
        out_shape=[jax.ShapeDtypeStruct((h, rq, c), F32), jax.ShapeDtypeStruct((h, c, d), F32)],
        name="dsa_fold",
        compiler_params=_params("arbitrary"),
    )(wuq, w_uk, w_uv, wo)
    return (wq.transpose(1, 0, 2).reshape(rq, h * c).astype(BF16),
            wvo.reshape(h * c, d).astype(BF16))


def _dsa_proj_body(x_ref, sh_ref, sc_ref, gain_ref, win_ref, qn_ref, kvn_ref, kg_ref, kb_ref,
                   wq_ref, wqi_ref, qabs_ref, qidx_ref, aux_ref, kpad_ref, caug_ref):
    tm = x_ref.shape[1]
    hn = _rms_mod(x_ref[0], gain_ref[...], sc_ref[0], sh_ref[0]).astype(BF16)
    proj = _dot(hn, win_ref[...])
    ql = _rms(proj[:, :A_Q_RANK], qn_ref[...]).astype(BF16)
    ck = _rms(proj[:, A_Q_RANK:A_Q_RANK + A_KV_RANK], kvn_ref[...])
    slab = proj[:, A_Q_RANK + A_KV_RANK:]
    lane = lax.broadcasted_iota(I32, (1, LANES), 1)
    is_k = lane < IDX_DIM
    mu = jnp.sum(jnp.where(is_k, slab, 0.0), axis=-1, keepdims=True) * (1.0 / IDX_DIM)
    dk = jnp.where(is_k, slab - mu, 0.0)
    var = jnp.sum(dk * dk, axis=-1, keepdims=True) * (1.0 / IDX_DIM)
    kpad_ref[0] = (dk * lax.rsqrt(var + EPS) * kg_ref[...] + kb_ref[...]).astype(BF16)
    aux_ref[0] = slab * (IDX_HEADS ** -0.5 * IDX_DIM ** -0.5)
    qabs_ref[0] = _dot(ql, wq_ref[...]).astype(BF16)
    qidx_ref[0] = _dot(ql, wqi_ref[...]).astype(BF16)
    pos = pl.program_id(1) * tm + lax.broadcasted_iota(I32, (tm, 1), 0)
    p_hi = (pos >> 6).astype(F32)
    p_lo = (pos & 63).astype(F32)
    extra = jnp.where(lane < 2, p_hi, jnp.where(lane < 4, p_lo, 0.0))
    caug_ref[0, :, :A_KV_RANK] = ck.astype(BF16)
    caug_ref[0, :, A_KV_RANK:] = extra.astype(BF16)


def _dsa_proj(x, sh, sc, gain, w_in, q_norm, kv_norm, kidx_g, kidx_b, wq, w_qi, tm=512):
    b, s, d = x.shape
    tm = min(tm, s)
    n_in = w_in.shape[1]
    win = jnp.pad(w_in, ((0, 0), (0, 4 * LANES - n_in))).astype(BF16)
    wqi = jnp.pad(w_qi.reshape(A_Q_RANK, IDX_HEADS, IDX_DIM),
                  ((0, 0), (0, 0), (0, LANES - IDX_DIM))).reshape(A_Q_RANK, IDX_HEADS * LANES).astype(BF16)
    kg = jnp.pad(kidx_g, (0, LANES - IDX_DIM)).reshape(1, LANES)
    kb = jnp.pad(kidx_b, (0, LANES - IDX_DIM)).reshape(1, LANES)
    row = pl.BlockSpec((1, 1, d), lambda bi, i: (bi, 0, 0))
    tile = lambda n: pl.BlockSpec((1, tm, n), lambda bi, i: (bi, i, 0))
    nq, ni = A_HEADS * A_KV_RANK, IDX_HEADS * LANES
    return pl.pallas_call(
        _dsa_proj_body,
        grid=(b, s // tm),
        in_specs=[tile(d), row, row, _full((1, d)), _full(win.shape), _full((1, A_Q_RANK)),
                  _full((1, A_KV_RANK)), _full((1, LANES)), _full((1, LANES)),
                  _full(wq.shape), _full(wqi.shape)],
        out_specs=[tile(nq), tile(ni), tile(LANES), tile(LANES), tile(2 * A_KV_RANK)],
        out_shape=[jax.ShapeDtypeStruct((b, s, nq), BF16), jax.ShapeDtypeStruct((b, s, ni), BF16),
                   jax.ShapeDtypeStruct((b, s, LANES), F32), jax.ShapeDtypeStruct((b, s, LANES), BF16),
                   jax.ShapeDtypeStruct((b, s, 2 * A_KV_RANK), BF16)],
        name="dsa_proj",
        compiler_params=_params("arbitrary", "arbitrary"),
    )(x, sh, sc, gain.reshape(1, d), win, q_norm.reshape(1, -1), kv_norm.reshape(1, -1), kg, kb, wq, wqi)


def _bf16_digits(v):
    hi = float(np.asarray(v, np.float32).astype(BF16).astype(np.float32))
    lo = float(np.asarray(v - hi, np.float32).astype(BF16).astype(np.float32))
    return hi, lo


def _dsa_attn_body(x_ref, gm_ref, qabs_ref, qidx_ref, aux_ref, kpad_ref, caug_ref, wvo_ref, o_ref,
                   keys_ref, keyst_ref, qaug_ref, qis_ref, wb_ref, bias_ref, ahead_ref, t_ref, m_ref, al_ref,
                   l_ref, acc_ref, s_ref, p_ref):
    tq = x_ref.shape[1]
    seq = kpad_ref.shape[1]
    c = A_KV_RANK
    nj = KB // LANES
    q0 = pl.program_id(1) * tq
    kend = q0 + tq
    n_kb = (kend + KB - 1) // KB
    tail = (n_kb - 1) * KB
    lane = lax.broadcasted_iota(I32, (1, LANES), 1)
    trow = lax.broadcasted_iota(I32, (tq, 1), 0)
    slopes = [LOG2E * 2.0 ** (-8.0 * (h + 1) / A_HEADS) for h in range(A_HEADS)]

    for h in range(A_HEADS):
        rows = slice(h * tq, (h + 1) * tq)
        hi, lo = _bf16_digits(slopes[h])
        digits = jnp.where(lane == 0, 64.0 * hi, jnp.where(lane == 1, 64.0 * lo,
                           jnp.where(lane == 2, hi, jnp.where(lane == 3, lo, 0.0))))
        qaug_ref[rows, :c] = qabs_ref[0, :, h * c:(h + 1) * c]
        qaug_ref[rows, c:] = jnp.broadcast_to(digits, (tq, LANES)).astype(BF16)
    aux = aux_ref[0]
    for h in range(IDX_HEADS):
        qis_ref[h * tq:(h + 1) * tq, :] = qidx_ref[0, :, h * LANES:(h + 1) * LANES]
        wb_ref[h] = jnp.broadcast_to(aux[:, IDX_DIM + h:IDX_DIM + h + 1], (tq, LANES))

    qchunk = (q0 + trow) >> 6

    def to_key(sc):
        bits = pltpu.bitcast(sc, I32)
        return bits ^ ((bits >> 31) & 0x7FFFFFFF)

    def score_block(kb, carry):
        off = pl.multiple_of(kb * KB, KB)
        z = _dot_nt(qis_ref[...], kpad_ref[0, pl.ds(off, KB), :])
        for j in range(nj):
            sc = jnp.zeros((tq, LANES), F32)
            for h in range(IDX_HEADS):
                sc = sc + wb_ref[h] * jnp.maximum(z[h * tq:(h + 1) * tq, j * LANES:(j + 1) * LANES], 0.0)
            kchunk = (off + j * LANES + lane) >> 6
            sc = jnp.where(kchunk <= qchunk, sc, -jnp.inf)
            keys_ref[kb * nj + j] = to_key(sc)
            keyst_ref[pl.ds(pl.multiple_of(off + j * LANES, LANES), LANES), :] = to_key(sc.T)
        return carry

    lax.fori_loop(0, n_kb, score_block, 0)

    slab = 8 * SUBLANES
    krow = lax.broadcasted_iota(I32, (slab, LANES), 0)

    def count(pred):
        def body(kb, acc):
            for u in range(KB // slab):
                r0 = pl.multiple_of(kb * KB + u * slab, slab)
                acc = jnp.where(pred(keyst_ref[pl.ds(r0, slab), :], r0 + krow), acc + 1.0, acc)
            return acc
        part = lax.fori_loop(0, n_kb, body, jnp.zeros((slab, LANES), F32))
        return jnp.sum(part, axis=0, keepdims=True)

    def per_row(v):
        return pltpu.bitcast(pltpu.bitcast(jnp.broadcast_to(v, (LANES, LANES)), F32).T, I32)

    t_ref[...] = jnp.full(t_ref.shape, INT_MIN, I32)

    @pl.when(kend > TOPK_MAX)
    def _():
        kf = float(TOPK_MAX)
        n_pos = count(lambda k, s: k >= 0)
        t0 = jnp.where(n_pos >= kf, 0, INT_MIN).astype(I32)
        n0 = jnp.where(n_pos >= kf, n_pos, (n_kb * KB).astype(F32))

        def bit_step(i, carry):
            t, n_t = carry
            cand = t | jnp.left_shift(jnp.int32(1), 30 - i)
            n_c = count(lambda k, s: k >= cand)
            return jnp.where(n_c >= kf, cand, t), jnp.where(n_c >= kf, n_c, n_t)

        t, n_ge = lax.fori_loop(0, 31, bit_step, (t0, n0))
        t_ref[...] = per_row(t)

        @pl.when(jnp.max(n_ge) > kf)
        def _():
            need = kf - count(lambda k, s: k > t)
            n_bits = max(1, int(seq - 1).bit_length())

            def idx_step(i, x):
                cand = x | jnp.left_shift(jnp.int32(1), n_bits - 1 - i)
                below = count(lambda k, s: (k == t) & (s < cand))
                return jnp.where(below < need, cand, x)

            last = per_row(lax.fori_loop(0, n_bits, idx_step, jnp.zeros((1, LANES), I32)))
            thr_ = t_ref[...]
            demoted = jnp.where(thr_ == INT_MIN, thr_, thr_ - 1)

            def demote(j, carry):
                k = keys_ref[j]
                keys_ref[j] = jnp.where((k == thr_) & (j * LANES + lane > last), demoted, k)
                return carry

            lax.fori_loop(0, n_kb * nj, demote, 0)

    m_ref[...] = jnp.full(m_ref.shape, NEG, F32)
    l_ref[...] = jnp.zeros_like(l_ref)
    acc_ref[...] = jnp.zeros_like(acc_ref)
    thr = t_ref[...]
    hg = A_HEADS // ATT_GROUPS
    group = lambda g: slice(g * hg * tq, (g + 1) * hg * tq)

    def attend(kb, slot, is_tail):
        cb = caug_ref[0, pl.ds(pl.multiple_of(kb * KB, KB), KB), :]
        for g in range(ATT_GROUPS):
            s_ref[slot, group(g), :] = _dot_nt(qaug_ref[group(g), :], cb)
        for j in range(nj):
            cols = slice(j * LANES, (j + 1) * LANES)
            picked = keys_ref[kb * nj + j] >= thr
            if is_tail:
                kcol = tail + j * LANES + lane
                picked = picked & ((kcol >> 6) <= ((q0 + trow) >> 6))
                ahead_ref[:, cols] = -2.0 * jnp.maximum(kcol - (q0 + trow), 0).astype(F32)
            bias_ref[slot, :, cols] = jnp.where(picked, 0.0, NEG)
        for g in range(ATT_GROUPS):
            for h in range(g * hg, (g + 1) * hg):
                rows = slice(h * tq, (h + 1) * tq)
                tiles = []
                for j in range(nj):
                    cols = slice(j * LANES, (j + 1) * LANES)
                    t_ = s_ref[slot, rows, cols] + bias_ref[slot, :, cols]
                    tiles.append(t_ + slopes[h] * ahead_ref[:, cols] if is_tail else t_)
                top = functools.reduce(jnp.maximum, tiles)
                m_old = m_ref[rows, :]
                m_new = jnp.maximum(m_old, jnp.broadcast_to(jnp.max(top, axis=-1, keepdims=True), top.shape))
                ps = [jnp.exp2(t_ - m_new) for t_ in tiles]
                p_ref[slot, rows, :] = jnp.concatenate(ps, axis=1).astype(BF16)
                alpha = jnp.exp2(m_old - m_new)
                l_ref[rows, :] = l_ref[rows, :] * alpha + functools.reduce(jnp.add, ps)
                al_ref[rows, :] = alpha
                m_ref[rows, :] = m_new
            acc_ref[group(g), :] = (acc_ref[group(g), :] * al_ref[group(g), :]
                                    + _dot(p_ref[slot, group(g), :], cb[:, :c]))

    n_past = n_kb - 1

    def past_pair(i, carry):
        attend(2 * i, 0, False)
        attend(2 * i + 1, 1, False)
        return carry

    lax.fori_loop(0, n_past // 2, past_pair, 0)

    @pl.when(n_past % 2 == 1)
    def _():
        attend(n_past - 1, 0, False)

    attend(n_past, 1, True)

    y = jnp.zeros((tq, x_ref.shape[2]), F32)
    for pair in range(A_HEADS // 2):
        parts = []
        for h in (2 * pair, 2 * pair + 1):
            rows = slice(h * tq, (h + 1) * tq)
            parts.append((acc_ref[rows, :] / jnp.sum(l_ref[rows, :], axis=-1, keepdims=True)).astype(BF16))
        y = y + _dot(jnp.concatenate(parts, axis=1), wvo_ref[2 * pair * c:(2 * pair + 2) * c, :])
    o_ref[0] = x_ref[0] + gm_ref[0] * y


def _dsa_attn(x, gm, qabs, qidx, aux, kpad, caug, wvo):
    b, s, d = x.shape
    tq = TQ
    assert tq == LANES and s % KB == 0
    hq = A_HEADS * tq
    row = pl.BlockSpec((1, 1, d), lambda bi, i: (bi, 0, 0))
    tile = lambda n: pl.BlockSpec((1, tq, n), lambda bi, i: (bi, i, 0))
    per_batch = lambda n: pl.BlockSpec((1, s, n), lambda bi, i: (bi, 0, 0))
    return pl.pallas_call(
        _dsa_attn_body,
        grid=(b, s // tq),
        in_specs=[tile(d), row, tile(qabs.shape[2]), tile(qidx.shape[2]), tile(LANES),
                  per_batch(LANES), per_batch(2 * A_KV_RANK), _full(wvo.shape)],
        out_specs=tile(d),
        out_shape=jax.ShapeDtypeStruct((b, s, d), F32),
        scratch_shapes=[pltpu.VMEM((s // LANES, tq, LANES), I32),
                        pltpu.VMEM((s, tq), I32),
                        pltpu.VMEM((hq, 2 * A_KV_RANK), BF16),
                        pltpu.VMEM((IDX_HEADS * tq, LANES), BF16),
                        pltpu.VMEM((IDX_HEADS, tq, LANES), F32),
                        pltpu.VMEM((2, tq, KB), F32),
                        pltpu.VMEM((tq, KB), F32),
                        pltpu.VMEM((tq, LANES), I32),
                        pltpu.VMEM((hq, LANES), F32),
                        pltpu.VMEM((hq, LANES), F32),
                        pltpu.VMEM((hq, LANES), F32),
                        pltpu.VMEM((hq, A_KV_RANK), F32),
                        pltpu.VMEM((2, hq, KB), F32),
                        pltpu.VMEM((2, hq, KB), BF16)],
        name="dsa_attn",
        compiler_params=_params("arbitrary", "arbitrary"),
    )(x, gm, qabs, qidx, aux, kpad, caug, wvo)


def _rwkv_proj_body(h_ref, sh_ref, sc_ref, gain_ref, mu_ref, wr_ref, wk_ref, wv_ref, w1_ref, w2_ref,
                    a1_ref, a2_ref, g1_ref, g2_ref, w0_ref, a0_ref,
                    r_ref, k_ref, v_ref, lw_ref, a_ref, g_ref, carry_ref):
    tm = h_ref.shape[1]

    @pl.when(pl.program_id(1) == 0)
    def _():
        carry_ref[...] = jnp.zeros_like(carry_ref)

    hn = _rms_mod(h_ref[0], gain_ref[...], sc_ref[0], sh_ref[0])
    prev = carry_ref[SUBLANES - 1:SUBLANES, :]
    carry_ref[...] = hn[tm - SUBLANES:, :]
    rid = lax.broadcasted_iota(I32, (tm, 1), 0)
    dx = jnp.where(rid == 0, prev, pltpu.roll(hn, 1, 0)) - hn
    mix = lambda j: (hn + dx * mu_ref[j:j + 1, :]).astype(BF16)

    def put(ref, val):
        for p in range(val.shape[1] // LANES):
            ref[0, p] = val[:, p * LANES:(p + 1) * LANES].astype(ref.dtype)

    put(r_ref, _dot(mix(0), wr_ref[...]))
    put(k_ref, _dot(mix(2), wk_ref[...]))
    put(v_ref, _dot(mix(3), wv_ref[...]))
    ww = w0_ref[...] + _dot(jnp.tanh(_dot(mix(1), w1_ref[...])).astype(BF16), w2_ref[...])
    softplus = jnp.maximum(-ww, 0.0) + jnp.log(1.0 + jnp.exp(-jnp.abs(ww)))
    put(lw_ref, -jnp.exp(-softplus - 0.5))
    put(a_ref, _sigmoid(a0_ref[...] + _dot(_dot(mix(4), a1_ref[...]).astype(BF16), a2_ref[...])))
    g_ref[0] = _dot(_sigmoid(_dot(mix(5), g1_ref[...])).astype(BF16), g2_ref[...]).astype(g_ref.dtype)


def _pad_rank(w_down, w_up):
    r = w_down.shape[1]
    p = -r % LANES
    return (jnp.pad(w_down, ((0, 0), (0, p))).astype(BF16), jnp.pad(w_up, ((0, p), (0, 0))).astype(BF16))


def _rwkv_proj(h, sh, sc, gain, mu, w_r, w_k, w_v, w0, w1, w2, a0, a1, a2, g1, g2, tm=512):
    b, s, d = h.shape
    tm = min(tm, s)
    w1p, w2p = _pad_rank(w1, w2)
    a1p, a2p = _pad_rank(a1, a2)
    g1p, g2p = _pad_rank(g1, g2)
    row = pl.BlockSpec((1, 1, d), lambda bi, i: (bi, 0, 0))
    tile = pl.BlockSpec((1, tm, d), lambda bi, i: (bi, i, 0))
    big = [w.astype(BF16) for w in (w_r, w_k, w_v)]
    small = [w1p, w2p, a1p, a2p, g1p, g2p]
    np_ = d // LANES
    ptile = pl.BlockSpec((1, np_, tm, LANES), lambda bi, i: (bi, 0, i, 0))
    pout = lambda dt: jax.ShapeDtypeStruct((b, np_, s, LANES), dt)
    return pl.pallas_call(
        _rwkv_proj_body,
        grid=(b, s // tm),
        in_specs=[tile, row, row, _full((1, d)), _full(mu.shape)] + [_full(w.shape) for w in big + small]
                 + [_full((1, d)), _full((1, d))],
        out_specs=[ptile] * 5 + [tile],
        out_shape=[pout(BF16), pout(BF16), pout(BF16), pout(F32), pout(BF16),
                   jax.ShapeDtypeStruct((b, s, d), BF16)],
        scratch_shapes=[pltpu.VMEM((SUBLANES, d), F32)],
        name="rwkv_proj",
        compiler_params=_params("arbitrary", "arbitrary"),
    )(h, sh, sc, gain.reshape(1, d), mu, *big, *small, w0.reshape(1, d), a0.reshape(1, d))


def _rwkv_scan_body(n_chunks, r_ref, k_ref, v_ref, lw_ref, a_ref, kk_ref, ka_ref, rk_ref, gg_ref, gb_ref,
                    y_ref, st_ref, w_s, rt_s, u0_s, y0_s, bon_s, arb_s, bkt_s, plt_s):
    L = SCAN_L
    P2 = 2 * L
    n_pairs = r_ref.shape[1]

    @pl.when(pl.program_id(1) == 0)
    def _():
        st_ref[...] = jnp.zeros_like(st_ref)

    lane = lax.broadcasted_iota(I32, (1, LANES), 1)
    head0 = lane < B_HEAD
    m0 = jnp.where(head0, 1.0, 0.0)
    m1 = 1.0 - m0
    ri = lax.broadcasted_iota(I32, (P2, P2), 0)
    ci = lax.broadcasted_iota(I32, (P2, P2), 1)
    same = (ri >> 6) == (ci >> 6)
    strict = same & ((ci & 63) < (ri & 63))
    incl = same & ((ci & 63) <= (ri & 63))
    eye = jnp.where(ri == ci, 1.0, 0.0)
    tril = jnp.where(lax.broadcasted_iota(I32, (L, L), 1) <= lax.broadcasted_iota(I32, (L, L), 0),
                     1.0, 0.0).astype(BF16)
    def seg_sum(x):
        s0 = jnp.sum(x * m0, axis=-1, keepdims=True)
        s1 = jnp.sum(x * m1, axis=-1, keepdims=True)
        return jnp.where(head0, s0, s1)

    stack = lambda x: jnp.concatenate([x * m0, x * m1], axis=0)
    fold = lambda x: x[:L] + x[L:]

    bdot = lambda a, b: _dot(a.astype(BF16), b.astype(BF16))

    each = lambda f, *xs: [f(*a) for a in zip(*xs)]

    def prepare_pair(p, carry):
        k_k, k_a, r_k = kk_ref[p], ka_ref[p], rk_ref[p]
        cs = range(n_chunks)
        r, kraw, v, lw, asig = ([ref[0, p, c * L:(c + 1) * L, :].astype(F32) for c in cs]
                                for ref in (r_ref, k_ref, v_ref, lw_ref, a_ref))
        kk = each(lambda x: x * k_k, kraw)
        kkn = each(lambda x: x / jnp.maximum(jnp.sqrt(seg_sum(x * x)), 1e-12), kk)
        kmod = each(lambda x, a: x * (1.0 + (a - 1.0) * k_a), kraw, asig)
        bvec = each(jnp.multiply, kkn, asig)
        gc = each(lambda x: _dot(tril, jnp.concatenate(_split(x), axis=1)), lw)
        g = each(lambda x: x[:, :LANES] + x[:, LANES:], gc)
        g_last = each(lambda x: x[L - 1:L, :], g)
        at = each(lambda kn, x, w: -kn * jnp.exp(x - w), kkn, g, lw)
        rt = each(lambda x, y: x * jnp.exp(y), r, g)
        inv = each(lambda x: jnp.exp(-x), g)
        to_end = each(lambda x, y: jnp.exp(x - y), g_last, g)
        bk_end = each(lambda b_, k_, e: jnp.concatenate([b_ * e, k_ * e], axis=0), bvec, kmod, to_end)
        lhs = each(lambda a, b_: jnp.concatenate([stack(a), stack(b_)], axis=0).astype(BF16), at, rt)
        rhs = each(lambda b_, k_, i: jnp.concatenate([b_ * i, b_ * i, k_ * i, k_ * i], axis=0).astype(BF16),
                   bvec, kmod, inv)
        mm = each(_dot_nt, lhs, rhs)
        n_ab = each(lambda x: jnp.where(strict, x[:P2, :P2], 0.0), mm)
        a_kk = each(lambda x: jnp.concatenate([jnp.where(strict, x[:P2, P2:], 0.0),
                                               jnp.where(incl, x[P2:, P2:], 0.0)], axis=0), mm)
        a_rb = each(lambda x: jnp.where(incl, x[P2:, :P2], 0.0), mm)

        tmat = each(lambda x: eye + x, n_ab)
        pw = each(bdot, n_ab, n_ab)
        levels = int(L).bit_length() - 2
        for lvl in range(levels):
            if lvl < levels - 1:
                both = each(lambda q, t: bdot(q, jnp.concatenate([q, t], axis=1)), pw, tmat)
                tmat = each(lambda t, b_: t + b_[:, P2:], tmat, both)
                pw = each(lambda b_: b_[:, :P2], both)
            else:
                tmat = each(lambda t, q: t + bdot(q, t), tmat, pw)

        akv = each(lambda a, x: bdot(a, stack(x)), a_kk, v)
        tw = each(lambda t, a, x: bdot(t, jnp.concatenate([stack(a), stack(fold(x[:P2]))], axis=1)),
                  tmat, at, akv)
        for c in cs:
            w_s[p, c] = fold(tw[c][:, :LANES]).astype(BF16)
            rt_s[p, c] = rt[c].astype(BF16)
            u0_s[p, c] = fold(tw[c][:, LANES:])
            y0_s[p, c] = fold(akv[c][P2:])
            bon_s[p, c] = seg_sum(r[c] * kmod[c] * r_k) * v[c]
            arb_s[p, c] = a_rb[c].astype(BF16)
            bkt_s[p, c] = bk_end[c].T.astype(BF16)
            plt_s[p, c] = jnp.broadcast_to(jnp.exp(g_last[c]), (P2, LANES)).T
        return carry

    lax.fori_loop(0, n_pairs, prepare_pair, 0)

    def advance(c, carry):
        rows = pl.ds(pl.multiple_of(c * L, L), L)
        ps = range(n_pairs)
        st = [st_ref[p] for p in ps]
        ws = [_dot(jnp.concatenate([w_s[p, c], rt_s[p, c]], axis=0), st[p].astype(BF16)) for p in ps]
        u = [ws[p][:L] + u0_s[p, c] for p in ps]
        uv = [jnp.concatenate([u[p].astype(BF16), v_ref[0, p, rows, :]], axis=0) for p in ps]
        upd = [_dot(bkt_s[p, c], uv[p]) for p in ps]
        for p in ps:
            st_ref[p] = st[p] * plt_s[p, c] + jnp.where(same, upd[p], 0.0)
        au = [_dot(arb_s[p, c], stack(u[p]).astype(BF16)) for p in ps]
        for p in ps:
            y = ws[p][L:] + fold(au[p]) + y0_s[p, c]
            mu = seg_sum(y) * (1.0 / B_HEAD)
            dy = y - mu
            var = seg_sum(dy * dy) * (1.0 / B_HEAD)
            y_ref[0, p, rows, :] = (dy * lax.rsqrt(var + GN_EPS) * gg_ref[p] + gb_ref[p]
                                    + bon_s[p, c]).astype(y_ref.dtype)
        return carry

    lax.fori_loop(0, n_chunks, advance, 0)


def _rwkv_scan(r, k, v, lw, a, k_k, k_a, r_k, gn_g, gn_b, lb=512):
    b, n_pairs, s, _ = r.shape
    lb = min(lb, s)
    nc = lb // SCAN_L
    tile = pl.BlockSpec((1, n_pairs, lb, LANES), lambda bi, i: (bi, 0, i, 0))
    vec = lambda x: x.reshape(n_pairs, 1, LANES)
    per_chunk = lambda rows, dt: pltpu.VMEM((n_pairs, nc, rows, LANES), dt)
    return pl.pallas_call(
        functools.partial(_rwkv_scan_body, nc),
        grid=(b, s // lb),
        in_specs=[tile] * 5 + [_full((n_pairs, 1, LANES))] * 5,
        out_specs=tile,
        out_shape=jax.ShapeDtypeStruct(r.shape, BF16),
        scratch_shapes=[pltpu.VMEM((n_pairs, LANES, LANES), F32),
                        per_chunk(SCAN_L, BF16), per_chunk(SCAN_L, BF16),
                        per_chunk(SCAN_L, F32), per_chunk(SCAN_L, F32), per_chunk(SCAN_L, F32),
                        per_chunk(2 * SCAN_L, BF16), per_chunk(2 * SCAN_L, BF16),
                        per_chunk(2 * SCAN_L, F32)],
        name="rwkv_scan",
        compiler_params=_params("arbitrary", "arbitrary"),
    )(r, k, v, lw, a, vec(k_k), vec(k_a), vec(r_k), vec(gn_g), vec(gn_b))


def kernel(x, c, ada_w, ada_b, norm_mix, norm_ffn, norm_final, a_w_in, a_q_norm, a_kv_norm, a_kidx_g, a_kidx_b, a_w_uq, a_w_qi, a_w_uk, a_w_uv, a_w_o, b_mu, b_w_r, b_w_k, b_w_v, b_w_o, b_w0, b_w1, b_w2, b_a0, b_a1, b_a2, b_g1, b_g2, b_k_k, b_k_a, b_r_k, b_gn_g, b_gn_b, f_w_up, f_conv_w, f_conv_b, f_w_down):
    b, s, d = x.shape
    mod = _ada(c, ada_w, ada_b).reshape(ada_w.shape[0], b, 6, 1, d)
    sh_m, sc_m, g_m, sh_f, sc_f, g_f = (mod[:, :, j] for j in range(6))

    wq, wvo = _fold(a_w_uq[0], a_w_uk[0], a_w_uv[0], a_w_o[0])
    qabs, qidx, aux, kpad, caug = _dsa_proj(x, sh_m[0], sc_m[0], norm_mix[0], a_w_in[0], a_q_norm[0],
                                            a_kv_norm[0], a_kidx_g[0], a_kidx_b[0], wq, a_w_qi[0])
    h = _dsa_attn(x, g_m[0], qabs, qidx, aux, kpad, caug, wvo)
    h = _ffn(h, sh_f[0], sc_f[0], g_f[0], norm_ffn[0], f_w_up[0], f_conv_w[0], f_conv_b[0], f_w_down[0],
             norm_final, False)

    r, k, v, lw, a, g = _rwkv_proj(h, sh_m[1], sc_m[1], norm_mix[1], b_mu[0], b_w_r[0], b_w_k[0], b_w_v[0],
                                   b_w0[0], b_w1[0], b_w2[0], b_a0[0], b_a1[0], b_a2[0], b_g1[0], b_g2[0])
    y = _rwkv_scan(r, k, v, lw, a, b_k_k[0], b_k_a[0], b_r_k[0].reshape(-1), b_gn_g[0], b_gn_b[0])
    return _ffn(h, sh_f[1], sc_f[1], g_f[1], norm_ffn[1], f_w_up[1], f_conv_w[1], f_conv_b[1], f_w_down[1],
                norm_final, True, mixer=(g_m[1], y, g, b_w_o[0]))
```
